```python
import math
import jax, jax.numpy as jnp
from jax import lax
import numpy as np

D_MODEL = 2048
BATCH = 4
SEQ = 4096
DEPTH = 4

GRID_W = 64
CTX_LEN = 256
N_MIXERS = 3
DN_ALPHA = (2 * DEPTH) ** 0.25
DN_BETA = (8 * DEPTH) ** -0.25
LN_EPS = 1e-5
N_CONV_LAYERS = (DEPTH + 2) // 3
N_HGRN_LAYERS = (DEPTH + 1) // 3
N_SSD_LAYERS = DEPTH // 3

CONV_K = 31
CONV_SPLIT = D_MODEL // 2

HG_DK = 128
HG_HEADS = D_MODEL // HG_DK
HG_KD = HG_HEADS * HG_DK
HG_DV = D_MODEL // HG_HEADS
HG_CHUNK = 64

M_EXPAND = 2
M_DINNER = M_EXPAND * D_MODEL
M_HEADDIM = 64
M_HEADS = M_DINNER // M_HEADDIM
M_GROUPS = 8
M_HPG = M_HEADS // M_GROUPS
M_DSTATE = 128
M_CONV = 5
M_CONV_DIM = M_DINNER + 2 * M_GROUPS * M_DSTATE
M_IN_DIM = M_DINNER + M_CONV_DIM + 2 * M_HEADS
M_CHUNK = 64

PEER_HEADS = 8
PEER_NKEYS = 128
PEER_EXPERTS = PEER_NKEYS * PEER_NKEYS
PEER_DK = 128
PEER_TOPK = 16
PEER_BLOCK = 128

kernel_name = 'hybrid_conv_hgrn2_ssd_peer_diffusion_trunk'

F32 = jnp.float32


def layer_norm(x, g, b):
    xf = x.astype(F32)
    mu = jnp.mean(xf, -1, keepdims=True)
    var = jnp.mean(jnp.square(xf - mu), -1, keepdims=True)
    return ((xf - mu) * lax.rsqrt(var + LN_EPS) * g + b).astype(x.dtype)


def rms_norm(x, eps=1e-6):
    xf = x.astype(F32)
    return xf * lax.rsqrt(jnp.mean(jnp.square(xf), -1, keepdims=True) + eps)


def dwconv1d(u, w):
    k = w.shape[0]
    return lax.conv_general_dilated(u, w[:, None, :].astype(u.dtype), (1,), [(k // 2, k // 2)],
                                    dimension_numbers=('NWC', 'WIO', 'NWC'),
                                    feature_group_count=u.shape[-1])


def axial_dwconv(u, w):
    b, n, ch = u.shape
    rows = n // GRID_W
    uh = u[..., :CONV_SPLIT].reshape(b * rows, GRID_W, CONV_SPLIT)
    yh = dwconv1d(uh, w[:, :CONV_SPLIT]).reshape(b, n, CONV_SPLIT)
    cv = ch - CONV_SPLIT
    uv = u[..., CONV_SPLIT:].reshape(b, rows, GRID_W, cv).transpose(0, 2, 1, 3).reshape(b * GRID_W, rows, cv)
    yv = dwconv1d(uv, w[:, CONV_SPLIT:]).reshape(b, GRID_W, rows, cv).transpose(0, 2, 1, 3).reshape(b, n, cv)
    return jnp.concatenate([yh, yv], axis=-1)


def conv_mixer(h, w_in, w_dw, ln_g, ln_b, w_out, on_grid):
    a = h @ w_in
    u = a[..., :D_MODEL] * jax.nn.sigmoid(a[..., D_MODEL:])
    u = axial_dwconv(u, w_dw) if on_grid else dwconv1d(u, w_dw)
    u = jax.nn.silu(layer_norm(u, ln_g, ln_b))
    return u @ w_out


def hgrn_chunk_scan(q, k, v, log_f, s0):
    b, h, l, _ = q.shape
    nc = l // HG_CHUNK

    def chunks(t):
        return jnp.moveaxis(t.reshape(b, h, nc, HG_CHUNK, t.shape[-1]), 2, 0)

    tri = jnp.tril(jnp.ones((HG_CHUNK, HG_CHUNK), bool))[:, :, None]

    def step(s, inp):
        qc, kc, vc, gc = inp
        bc = jnp.cumsum(gc, axis=2)
        rel = jnp.exp(jnp.where(tri, bc[:, :, :, None, :] - bc[:, :, None, :, :], -jnp.inf))
        att = jnp.einsum('bhtk,bhsk,bhtsk->bhts', qc, kc, rel)
        o = jnp.einsum('bhts,bhsv->bhtv', att, vc) + jnp.einsum('bhtk,bhkv->bhtv', qc * jnp.exp(bc), s)
        b_end = bc[:, :, -1:, :]
        s_new = s * jnp.exp(b_end[:, :, 0, :, None]) + jnp.einsum('bhsk,bhsv->bhkv', kc * jnp.exp(b_end - bc), vc)
        return s_new, o

    s_fin, o = lax.scan(step, s0, (chunks(q), chunks(k), chunks(v), chunks(log_f)))
    o = jnp.moveaxis(o, 0, 2).reshape(b, h, l, v.shape[-1])
    return o, s_fin


def hgrn_mixer(h_ctx, h_lat, w_q, w_f, w_i, w_g, norm_g, w_o, lb, ctx_out):
    def to_heads(t):
        b, l, _ = t.shape
        return t.reshape(b, l, HG_HEADS, -1).transpose(0, 2, 1, 3)

    def project(h):
        q = to_heads(jax.nn.silu(h @ w_q))
        v = to_heads(h @ w_i)
        dirs = []
        for d in range(2):
            z = to_heads(h @ w_f[d]).astype(F32)
            lbd = lb[d].reshape(HG_HEADS, 1, HG_DK)
            log_f = jnp.logaddexp(jnp.log(lbd), jnp.log1p(-lbd) + jax.nn.log_sigmoid(z))
            k = (1.0 - lbd) * jax.nn.sigmoid(-z)
            dirs.append((k, log_f))
        return q, v, dirs

    def flip(t):
        return jnp.flip(t, axis=2)

    def readout(o, h):
        b, _, l, _ = o.shape
        o = rms_norm(o.transpose(0, 2, 1, 3)).reshape(b, l, D_MODEL) * norm_g
        return (o * jax.nn.silu(h @ w_g)).astype(h.dtype) @ w_o

    qc, vc, ((kc0, gc0), (kc1, gc1)) = project(h_ctx)
    ql, vl, ((kl0, gl0), (kl1, gl1)) = project(h_lat)
    s0 = jnp.zeros((h_lat.shape[0], HG_HEADS, HG_DK, HG_DV), F32)
    oc_f, s_f = hgrn_chunk_scan(qc, kc0, vc, gc0, s0)
    ol_f, _ = hgrn_chunk_scan(ql, kl0, vl, gl0, s_f)
    oc_b, s_b = hgrn_chunk_scan(flip(qc), flip(kc1), flip(vc), flip(gc1), s0)
    ol_b, _ = hgrn_chunk_scan(flip(ql), flip(kl1), flip(vl), flip(gl1), s_b)
    y_lat = readout(ol_f + flip(ol_b), h_lat)
    y_ctx = readout(oc_f + flip(oc_b), h_ctx) if ctx_out else None
    return y_ctx, y_lat


def ssd_chunked(xdt, da, bm, cm, s0):
    b, l = xdt.shape[:2]
    nc = l // M_CHUNK
    X = xdt.reshape(b, nc, M_CHUNK, M_GROUPS, M_HPG, M_HEADDIM)
    A = jnp.cumsum(da.reshape(b, nc, M_CHUNK, M_GROUPS, M_HPG), axis=2)
    Bc = bm.reshape(b, nc, M_CHUNK, M_GROUPS, M_DSTATE)
    Cc = cm.reshape(b, nc, M_CHUNK, M_GROUPS, M_DSTATE)
    tri = jnp.tril(jnp.ones((M_CHUNK, M_CHUNK), bool))[:, :, None, None]
    seg = jnp.exp(jnp.where(tri, A[:, :, :, None] - A[:, :, None, :], -jnp.inf))
    cb = jnp.einsum('bclgn,bcsgn->bclsg', Cc, Bc)
    y_diag = jnp.einsum('bclsgj,bcsgjp->bclgjp', cb[..., None] * seg, X)
    a_end = A[:, :, -1]
    states = jnp.einsum('bclgn,bclgjp->bcgjpn', Bc, X * jnp.exp(a_end[:, :, None] - A)[..., None])

    def step(s, inp):
        st, dec = inp
        return s * dec[..., None, None] + st, s

    s_fin, s_prev = lax.scan(step, s0, (jnp.moveaxis(states, 1, 0), jnp.moveaxis(jnp.exp(a_end), 1, 0)))
    s_prev = jnp.moveaxis(s_prev, 0, 1)
    y_off = jnp.einsum('bclgn,bcgjpn->bclgjp', Cc, s_prev) * jnp.exp(A)[..., None]
    return (y_diag + y_off).reshape(b, l, M_GROUPS, M_HPG, M_HEADDIM), s_fin


def ssd_mixer(h_ctx, h_lat, w_in, conv_w, conv_b, dt_bias, a_log, d_skip, norm_g, w_out, ctx_out):
    def project(h):
        b, l, _ = h.shape
        zxbcdt = h @ w_in
        z = zxbcdt[..., :M_DINNER]
        xbc = jax.nn.silu(dwconv1d(zxbcdt[..., M_DINNER:M_DINNER + M_CONV_DIM], conv_w) + conv_b)
        dt = zxbcdt[..., M_DINNER + M_CONV_DIM:].astype(F32).reshape(b, l, 2, M_GROUPS, M_HPG)
        xs = xbc[..., :M_DINNER].reshape(b, l, M_GROUPS, M_HPG, M_HEADDIM)
        bm = xbc[..., M_DINNER:M_DINNER + M_GROUPS * M_DSTATE].reshape(b, l, M_GROUPS, M_DSTATE)
        cm = xbc[..., M_DINNER + M_GROUPS * M_DSTATE:].reshape(b, l, M_GROUPS, M_DSTATE)
        return z, xs, bm, cm, dt

    a = -jnp.exp(a_log.astype(F32)).reshape(2, M_GROUPS, M_HPG)
    dtb = dt_bias.astype(F32).reshape(2, M_GROUPS, M_HPG)

    def scan_dir(d, xs, bm, cm, dt_raw, s0):
        dt = jax.nn.softplus(dt_raw[:, :, d] + dtb[d])
        return ssd_chunked(xs * dt[..., None], dt * a[d], bm, cm, s0)

    def flip(t):
        return jnp.flip(t, axis=1)

    def readout(y, xs, z):
        b, l = z.shape[:2]
        y = (y + d_skip.reshape(M_GROUPS, M_HPG)[..., None] * xs).reshape(b, l, M_DINNER)
        y = rms_norm((y * jax.nn.silu(z)).reshape(b, l, M_GROUPS, -1)).reshape(b, l, M_DINNER) * norm_g
        return y.astype(z.dtype) @ w_out

    zc, xc, bc, cc, dtc = project(h_ctx)
    zl, xl, bl, cl, dtl = project(h_lat)
    s0 = jnp.zeros((h_lat.shape[0], M_GROUPS, M_HPG, M_HEADDIM, M_DSTATE), F32)
    yc_f, s_f = scan_dir(0, xc, bc, cc, dtc, s0)
    yl_f, _ = scan_dir(0, xl, bl, cl, dtl, s_f)
    yc_b, s_b = scan_dir(1, flip(xc), flip(bc), flip(cc), flip(dtc), s0)
    yl_b, _ = scan_dir(1, flip(xl), flip(bl), flip(cl), flip(dtl), s_b)
    y_lat = readout(yl_f + flip(yl_b), xl, zl)
    y_ctx = readout(yc_f + flip(yc_b), xc, zc) if ctx_out else None
    return y_ctx, y_lat


def peer_ffn(h, w_q, keys, u, v):
    b, l, _ = h.shape
    t = h.reshape(b * l, D_MODEL)
    q = (t @ w_q).reshape(-1, PEER_HEADS, 2, PEER_DK // 2)
    s = jnp.einsum('thpd,hpnd->thpn', q, keys).astype(F32)
    sv, si = lax.top_k(s, PEER_TOPK)
    cand_s = (sv[:, :, 0, :, None] + sv[:, :, 1, None, :]).reshape(-1, PEER_HEADS, PEER_TOPK * PEER_TOPK)
    cand_i = (si[:, :, 0, :, None] * PEER_NKEYS + si[:, :, 1, None, :]).reshape(-1, PEER_HEADS, PEER_TOPK * PEER_TOPK)
    top_s, pos = lax.top_k(cand_s, PEER_TOPK)
    idx = jnp.take_along_axis(cand_i, pos, axis=-1).reshape(-1, PEER_HEADS * PEER_TOPK)
    gate = jax.nn.softmax(top_s, axis=-1).reshape(-1, PEER_HEADS * PEER_TOPK)
    nb = t.shape[0] // PEER_BLOCK

    def block(args):
        tb, ib, gb = args
        act = jax.nn.gelu(jnp.einsum('tkd,td->tk', u[ib], tb).astype(F32), approximate=False)
        return jnp.einsum('tk,tkd->td', (gb * act).astype(tb.dtype), v[ib])

    out = lax.map(block, (t.reshape(nb, PEER_BLOCK, D_MODEL), idx.reshape(nb, PEER_BLOCK, -1),
                          gate.reshape(nb, PEER_BLOCK, -1)))
    return out.reshape(b, l, D_MODEL)


def setup_inputs(seed: int = 0) -> dict:
    key = jax.random.key(seed)
    ks = iter(jax.random.split(key, 48))

    def nrm(shape, scale):
        return jax.random.normal(next(ks), shape, jnp.float32) * scale

    D = D_MODEL
    dt0 = jnp.exp(jax.random.uniform(next(ks), (N_SSD_LAYERS, 2, M_HEADS), jnp.float32,
                                     math.log(1e-3), math.log(1e-1)))
    return {
        'x': nrm((BATCH, SEQ, D), 1.0),
        'c': nrm((BATCH, D), 1.0),
        'ctx': nrm((BATCH, CTX_LEN, D), 1.0),
        'c_ctx': nrm((D,), 1.0),
        'mod_w': nrm((DEPTH, D, 6 * D), D ** -0.5),
        'mod_b': nrm((DEPTH, 6 * D), 0.02),
        'ln_g': 1.0 + nrm((DEPTH, 2, D), 0.02),
        'ln_b': nrm((DEPTH, 2, D), 0.02),
        'conv_w_in': nrm((N_CONV_LAYERS, D, 2 * D), D ** -0.5),
        'conv_w_dw': nrm((N_CONV_LAYERS, CONV_K, D), CONV_K ** -0.5),
        'conv_ln_g': 1.0 + nrm((N_CONV_LAYERS, D), 0.02),
        'conv_ln_b': nrm((N_CONV_LAYERS, D), 0.02),
        'conv_w_out': nrm((N_CONV_LAYERS, D, D), DN_BETA * D ** -0.5),
        'hg_w_q': nrm((N_HGRN_LAYERS, D, HG_KD), D ** -0.5),
        'hg_w_f': nrm((N_HGRN_LAYERS, 2, D, HG_KD), D ** -0.5),
        'hg_w_i': nrm((N_HGRN_LAYERS, D, D), D ** -0.5),
        'hg_w_g': nrm((N_HGRN_LAYERS, D, D), D ** -0.5),
        'hg_norm_g': 1.0 + nrm((N_HGRN_LAYERS, D), 0.02),
        'hg_w_o': nrm((N_HGRN_LAYERS, D, D), DN_BETA * D ** -0.5),
        'hg_lb_logits': nrm((2, DEPTH, HG_KD), 0.1),
        'm_w_in': nrm((N_SSD_LAYERS, D, M_IN_DIM), D ** -0.5),
        'm_conv_w': nrm((N_SSD_LAYERS, M_CONV, M_CONV_DIM), M_CONV ** -0.5),
        'm_conv_b': nrm((N_SSD_LAYERS, M_CONV_DIM), 0.02),
        'm_dt_bias': dt0 + jnp.log(-jnp.expm1(-dt0)),
        'm_A_log': jnp.log(jax.random.uniform(next(ks), (N_SSD_LAYERS, 2, M_HEADS), jnp.float32, 1.0, 16.0)),
        'm_D': 1.0 + nrm((N_SSD_LAYERS, M_HEADS), 0.02),
        'm_norm_g': 1.0 + nrm((N_SSD_LAYERS, M_DINNER), 0.02),
        'm_w_out': nrm((N_SSD_LAYERS, M_DINNER, D), DN_BETA * M_DINNER ** -0.5),
        'peer_w_q': nrm((DEPTH, D, PEER_HEADS * PEER_DK), D ** -0.5),
        'peer_keys': nrm((DEPTH, PEER_HEADS, 2, PEER_NKEYS, PEER_DK // 2), (PEER_DK // 2) ** -0.5),
        'peer_u': nrm((DEPTH, PEER_EXPERTS, D), D ** -0.5),
        'peer_v': nrm((DEPTH, PEER_EXPERTS, D), DN_BETA * PEER_HEADS ** -0.5),
    }


def reference(x, c, ctx, c_ctx, mod_w, mod_b, ln_g, ln_b, conv_w_in, conv_w_dw, conv_ln_g, conv_ln_b,
              conv_w_out, hg_w_q, hg_w_f, hg_w_i, hg_w_g, hg_norm_g, hg_w_o, hg_lb_logits, m_w_in, m_conv_w,
              m_conv_b, m_dt_bias, m_A_log, m_D, m_norm_g, m_w_out, peer_w_q, peer_keys, peer_u, peer_v):
    reads_ctx = (False, True, True)
    last_ctx = max([i for i in range(DEPTH) if reads_ctx[i % N_MIXERS]], default=-1)
    lb_all = jnp.cumsum(jax.nn.softmax(hg_lb_logits.astype(F32), axis=1), axis=1)
    lb_all = lb_all - lb_all[:, :1]
    silu_c = jax.nn.silu(c)
    silu_cc = jax.nn.silu(c_ctx)
    for i in range(DEPTH):
        kind = i % N_MIXERS
        j = i // N_MIXERS
        run_ctx = i <= last_ctx
        upd_ctx = i < last_ctx
        m_lat = jnp.split((silu_c @ mod_w[i] + mod_b[i])[:, None, :], 6, axis=-1)
        m_ctx = jnp.split(silu_cc @ mod_w[i] + mod_b[i], 6, axis=-1)
        h_lat = x * (1.0 + m_lat[1]) + m_lat[0]
        h_ctx = ctx * (1.0 + m_ctx[1]) + m_ctx[0] if run_ctx else None
        if kind == 0:
            cw = (conv_w_in[j], conv_w_dw[j], conv_ln_g[j], conv_ln_b[j], conv_w_out[j])
            y_lat = conv_mixer(h_lat, *cw, on_grid=True)
            y_ctx = conv_mixer(h_ctx, *cw, on_grid=False) if upd_ctx else None
        elif kind == 1:
            y_ctx, y_lat = hgrn_mixer(h_ctx, h_lat, hg_w_q[j], hg_w_f[j], hg_w_i[j], hg_w_g[j], hg_norm_g[j],
                                      hg_w_o[j], lb_all[:, i], upd_ctx)
        else:
            y_ctx, y_lat = ssd_mixer(h_ctx, h_lat, m_w_in[j], m_conv_w[j], m_conv_b[j], m_dt_bias[j], m_A_log[j],
                                     m_D[j], m_norm_g[j], m_w_out[j], upd_ctx)
        pw = (peer_w_q[i], peer_keys[i], peer_u[i], peer_v[i])
        x = layer_norm(DN_ALPHA * x + m_lat[2] * y_lat, ln_g[i, 0], ln_b[i, 0])
        x = layer_norm(DN_ALPHA * x + m_lat[5] * peer_ffn(x * (1.0 + m_lat[4]) + m_lat[3], *pw),
                       ln_g[i, 1], ln_b[i, 1])
        if upd_ctx:
            ctx = layer_norm(DN_ALPHA * ctx + m_ctx[2] * y_ctx, ln_g[i, 0], ln_b[i, 0])
            ctx = layer_norm(DN_ALPHA * ctx + m_ctx[5] * peer_ffn(ctx * (1.0 + m_ctx[4]) + m_ctx[3], *pw),
                             ln_g[i, 1], ln_b[i, 1])
    return x
```

```python
import functools
import math

import jax
import jax.numpy as jnp
from jax import lax
from jax.experimental import pallas as pl
from jax.experimental.pallas import tpu as pltpu

F32 = jnp.float32
BF16 = jnp.bfloat16

GRID_W = 64
N_MIXERS = 3
LN_EPS = 1e-5
HG_DK = 128
HG_CHUNK = 64
M_HEADDIM = 64
M_GROUPS = 8
M_DSTATE = 128
M_CHUNK = 64
PEER_TOPK = 16

LANES = 128
SUBLANES = 8
VMEM_LIMIT_BYTES = 56 * 1024 * 1024


def _pick_tile(n, cap):
    if n <= cap:
        return n
    best = None
    for t in range(LANES, cap + 1, LANES):
        if n % t == 0:
            best = t
    assert best is not None, (n, cap)
    return best


def _mm_kernel(a_ref, w_ref, o_ref):
    a = a_ref[...].astype(BF16)
    w = w_ref[...].astype(BF16)
    o_ref[...] = jnp.dot(a, w, preferred_element_type=F32).astype(o_ref.dtype)


def _mm(a, w, out_dtype=F32, tm_cap=512, tn_cap=512):
    m, k = a.shape
    k2, n = w.shape
    assert k == k2
    tm = _pick_tile(m, tm_cap) if m % SUBLANES == 0 and m > tm_cap else m
    tn = _pick_tile(n, tn_cap)
    return pl.pallas_call(
        _mm_kernel,
        grid=(m // tm, n // tn),
        in_specs=[pl.BlockSpec((tm, k), lambda i, j: (i, 0)),
                  pl.BlockSpec((k, tn), lambda i, j: (0, j))],
        out_specs=pl.BlockSpec((tm, tn), lambda i, j: (i, j)),
        out_shape=jax.ShapeDtypeStruct((m, n), out_dtype),
        compiler_params=pltpu.CompilerParams(
            dimension_semantics=("parallel", "arbitrary"), vmem_limit_bytes=VMEM_LIMIT_BYTES),
        name="proj_mm",
    )(a, w)


def _proj(h, w):
    b, l, k = h.shape
    return _mm(h.reshape(b * l, k).astype(BF16), w.astype(BF16)).reshape(b, l, w.shape[1])


def _topk_desc(x, k):
    vals = []
    for a in range(k):
        m = jnp.max(x, axis=0, keepdims=True)
        vals.append(m)
        if a + 1 < k:
            x = jnp.where(x == m, -jnp.inf, x)
    return vals


def _peer_route_kernel(h_ref, wq_ref, kbd_ref, s2_ref, e2_ref, thr_ref, e1_ref, *, n_heads, n_keys):
    q = jnp.dot(h_ref[...], wq_ref[...], preferred_element_type=F32)
    st = lax.dot_general(kbd_ref[...], q.astype(BF16), (((1,), (1,)), ((), ())),
                         preferred_element_type=F32)
    for h in range(n_heads):
        s1 = st[(2 * h) * n_keys:(2 * h + 1) * n_keys, :]
        s2 = st[(2 * h + 1) * n_keys:(2 * h + 2) * n_keys, :]
        sv1 = _topk_desc(s1, PEER_TOPK)
        sv2 = _topk_desc(s2, PEER_TOPK)
        sv2_all = jnp.concatenate(sv2, axis=0)
        cand = jnp.concatenate([sv1[a] + sv2_all for a in range(PEER_TOPK)], axis=0)
        th = _topk_desc(cand, PEER_TOPK)[-1]
        top = sv1[0] + sv2[0]
        z = jnp.sum(jnp.where(cand >= th, jnp.exp(cand - top), 0.0), axis=0, keepdims=True)
        thr = jnp.full(s1.shape, jnp.inf, F32)
        for b in range(PEER_TOPK):
            thr = jnp.where(s1 + sv2[b] >= th, sv2[b], thr)
        s2_ref[h] = s2
        e2_ref[h] = jnp.exp(s2 - sv2[0])
        thr_ref[h] = thr
        e1_ref[h] = jnp.exp(s1 - sv1[0]) / z


def _peer_route(hm, wq, kbd, n_heads, n_keys, tm):
    t, d = hm.shape
    out_sd = jax.ShapeDtypeStruct((n_heads, n_keys, t), F32)
    out_spec = pl.BlockSpec((n_heads, n_keys, tm), lambda i: (0, 0, i))
    return pl.pallas_call(
        functools.partial(_peer_route_kernel, n_heads=n_heads, n_keys=n_keys),
        grid=(t // tm,),
        in_specs=[pl.BlockSpec((tm, d), lambda i: (i, 0)),
                  pl.BlockSpec(wq.shape, lambda i: (0, 0)),
                  pl.BlockSpec(kbd.shape, lambda i: (0, 0))],
        out_specs=[out_spec] * 4,
        out_shape=[out_sd] * 4,
        compiler_params=pltpu.CompilerParams(
            dimension_semantics=("parallel",), vmem_limit_bytes=VMEM_LIMIT_BYTES),
        name="peer_route",
    )(hm, wq, kbd)


def _gelu_exact(x):
    return 0.5 * x * (1.0 + lax.erf(x * (1.0 / math.sqrt(2.0))))


def _peer_ffn_kernel(x_ref, u_ref, v_ref, s2_ref, e2_ref, thr_ref, e1_ref, o_ref, ga_ref, *,
                     n_heads, n_keys, rows_per_step):
    @pl.when(pl.program_id(1) == 0)
    def _():
        o_ref[...] = jnp.zeros_like(o_ref)

    x = x_ref[...]
    for ii in range(rows_per_step):
        lo, hi = ii * n_keys, (ii + 1) * n_keys
        a = lax.dot_general(u_ref[lo:hi, :], x, (((1,), (1,)), ((), ())),
                            preferred_element_type=F32)
        gate = jnp.zeros(a.shape, F32)
        for h in range(n_heads):
            thr = thr_ref[h, ii:ii + 1, :]
            w = e1_ref[h, ii:ii + 1, :]
            gate = gate + jnp.where(s2_ref[h] >= thr, e2_ref[h], 0.0) * w
        ga_ref[lo:hi, :] = (gate * _gelu_exact(a)).astype(BF16)
    o_ref[...] += lax.dot_general(ga_ref[...], v_ref[...], (((0,), (0,)), ((), ())),
                                  preferred_element_type=F32)


def _peer_ffn_dense(hm, u, v, s2, e2, thr, e1, n_heads, n_keys, tm):
    t, d = hm.shape
    n_exp = u.shape[0]
    rows_per_step = SUBLANES
    ec = rows_per_step * n_keys
    tok_spec = pl.BlockSpec((n_heads, n_keys, tm), lambda i, e: (0, 0, i))
    row_spec = pl.BlockSpec((n_heads, rows_per_step, tm), lambda i, e: (0, e, i))
    return pl.pallas_call(
        functools.partial(_peer_ffn_kernel, n_heads=n_heads, n_keys=n_keys, rows_per_step=rows_per_step),
        grid=(t // tm, n_exp // ec),
        in_specs=[pl.BlockSpec((tm, d), lambda i, e: (i, 0)),
                  pl.BlockSpec((ec, d), lambda i, e: (e, 0)),
                  pl.BlockSpec((ec, d), lambda i, e: (e, 0)),
                  tok_spec, tok_spec, row_spec, row_spec],
        out_specs=pl.BlockSpec((tm, d), lambda i, e: (i, 0)),
        out_shape=jax.ShapeDtypeStruct((t, d), F32),
        scratch_shapes=[pltpu.VMEM((ec, tm), BF16)],
        compiler_params=pltpu.CompilerParams(
            dimension_semantics=("parallel", "arbitrary"), vmem_limit_bytes=VMEM_LIMIT_BYTES),
        name="peer_ffn",
    )(hm, u, v, s2, e2, thr, e1)


def _peer(h, w_q, keys, u, v):
    b, l, d = h.shape
    n_heads, _, n_keys, half = keys.shape
    hm = h.reshape(b * l, d).astype(BF16)
    eye = jnp.eye(2 * n_heads, dtype=F32)
    kflat = keys.reshape(2 * n_heads, n_keys, half)
    kbd = (eye[:, None, :, None] * kflat[:, :, None, :]).reshape(2 * n_heads * n_keys, 2 * n_heads * half)
    tm_route = _pick_tile(b * l, 256)
    s2, e2, thr, e1 = _peer_route(hm, w_q.astype(BF16), kbd.astype(BF16), n_heads, n_keys, tm_route)
    tm = _pick_tile(b * l, 512)
    out = _peer_ffn_dense(hm, u, v, s2, e2, thr, e1, n_heads, n_keys, tm)
    return out.reshape(b, l, d)


def _layer_norm(x, g, b):
    mu = jnp.mean(x, -1, keepdims=True)
    var = jnp.mean(jnp.square(x - mu), -1, keepdims=True)
    return (x - mu) * lax.rsqrt(var + LN_EPS) * g + b


def _rms_norm(x, eps=1e-6):
    return x * lax.rsqrt(jnp.mean(jnp.square(x), -1, keepdims=True) + eps)


def _dwconv1d(u, w):
    k = w.shape[0]
    return lax.conv_general_dilated(u, w[:, None, :], (1,), [(k // 2, k // 2)],
                                    dimension_numbers=('NWC', 'WIO', 'NWC'),
                                    feature_group_count=u.shape[-1])


def _axial_dwconv(u, w):
    b, n, ch = u.shape
    split = ch // 2
    rows = n // GRID_W
    uh = u[..., :split].reshape(b * rows, GRID_W, split)
    yh = _dwconv1d(uh, w[:, :split]).reshape(b, n, split)
    cv = ch - split
    uv = u[..., split:].reshape(b, rows, GRID_W, cv).transpose(0, 2, 1, 3).reshape(b * GRID_W, rows, cv)
    yv = _dwconv1d(uv, w[:, split:]).reshape(b, GRID_W, rows, cv).transpose(0, 2, 1, 3).reshape(b, n, cv)
    return jnp.concatenate([yh, yv], axis=-1)


def _conv_mixer(h, w_in, w_dw, ln_g, ln_b, w_out, on_grid):
    d = h.shape[-1]
    a = _proj(h, w_in)
    u = a[..., :d] * jax.nn.sigmoid(a[..., d:])
    u = _axial_dwconv(u, w_dw) if on_grid else _dwconv1d(u, w_dw)
    u = jax.nn.silu(_layer_norm(u, ln_g, ln_b))
    return _proj(u, w_out)


def _hgrn_chunk_scan(q, k, v, log_f, s0):
    b, h, l, _ = q.shape
    nc = l // HG_CHUNK

    def chunks(t):
        return jnp.moveaxis(t.reshape(b, h, nc, HG_CHUNK, t.shape[-1]), 2, 0)

    tri = jnp.tril(jnp.ones((HG_CHUNK, HG_CHUNK), bool))[:, :, None]

    def step(s, inp):
        qc, kc, vc, gc = inp
        bc = jnp.cumsum(gc, axis=2)
        rel = jnp.exp(jnp.where(tri, bc[:, :, :, None, :] - bc[:, :, None, :, :], -jnp.inf))
        att = jnp.einsum('bhtk,bhsk,bhtsk->bhts', qc, kc, rel)
        o = jnp.einsum('bhts,bhsv->bhtv', att, vc) + jnp.einsum('bhtk,bhkv->bhtv', qc * jnp.exp(bc), s)
        b_end = bc[:, :, -1:, :]
        s_new = s * jnp.exp(b_end[:, :, 0, :, None]) + jnp.einsum('bhsk,bhsv->bhkv', kc * jnp.exp(b_end - bc), vc)
        return s_new, o

    s_fin, o = lax.scan(step, s0, (chunks(q), chunks(k), chunks(v), chunks(log_f)))
    o = jnp.moveaxis(o, 0, 2).reshape(b, h, l, v.shape[-1])
    return o, s_fin


def _hgrn_mixer(h_ctx, h_lat, w_q, w_f, w_i, w_g, norm_g, w_o, lb, ctx_out):
    d = h_lat.shape[-1]
    n_heads = d // HG_DK

    def to_heads(t):
        b, l, _ = t.shape
        return t.reshape(b, l, n_heads, -1).transpose(0, 2, 1, 3)

    def project(h):
        q = to_heads(jax.nn.silu(_proj(h, w_q)))
        v = to_heads(_proj(h, w_i))
        dirs = []
        for dd in range(2):
            z = to_heads(_proj(h, w_f[dd]))
            lbd = lb[dd].reshape(n_heads, 1, HG_DK)
            log_f = jnp.logaddexp(jnp.log(lbd), jnp.log1p(-lbd) + jax.nn.log_sigmoid(z))
            k = (1.0 - lbd) * jax.nn.sigmoid(-z)
            dirs.append((k, log_f))
        return q, v, dirs

    def flip(t):
        return jnp.flip(t, axis=2)

    def readout(o, h):
        b, _, l, _ = o.shape
        o = _rms_norm(o.transpose(0, 2, 1, 3)).reshape(b, l, d) * norm_g
        return _proj(o * jax.nn.silu(_proj(h, w_g)), w_o)

    qc, vc, ((kc0, gc0), (kc1, gc1)) = project(h_ctx)
    ql, vl, ((kl0, gl0), (kl1, gl1)) = project(h_lat)
    s0 = jnp.zeros((h_lat.shape[0], n_heads, HG_DK, d // n_heads), F32)
    oc_f, s_f = _hgrn_chunk_scan(qc, kc0, vc, gc0, s0)
    ol_f, _ = _hgrn_chunk_scan(ql, kl0, vl, gl0, s_f)
    oc_b, s_b = _hgrn_chunk_scan(flip(qc), flip(kc1), flip(vc), flip(gc1), s0)
    ol_b, _ = _hgrn_chunk_scan(flip(ql), flip(kl1), flip(vl), flip(gl1), s_b)
    y_lat = readout(ol_f + flip(ol_b), h_lat)
    y_ctx = readout(oc_f + flip(oc_b), h_ctx) if ctx_out else None
    return y_ctx, y_lat


def _ssd_chunked(xdt, da, bm, cm, s0):
    b, l = xdt.shape[:2]
    hpg = xdt.shape[3]
    nc = l // M_CHUNK
    X = xdt.reshape(b, nc, M_CHUNK, M_GROUPS, hpg, M_HEADDIM)
    A = jnp.cumsum(da.reshape(b, nc, M_CHUNK, M_GROUPS, hpg), axis=2)
    Bc = bm.reshape(b, nc, M_CHUNK, M_GROUPS, M_DSTATE)
    Cc = cm.reshape(b, nc, M_CHUNK, M_GROUPS, M_DSTATE)
    tri = jnp.tril(jnp.ones((M_CHUNK, M_CHUNK), bool))[:, :, None, None]
    seg = jnp.exp(jnp.where(tri, A[:, :, :, None] - A[:, :, None, :], -jnp.inf))
    cb = jnp.einsum('bclgn,bcsgn->bclsg', Cc, Bc)
    y_diag = jnp.einsum('bclsgj,bcsgjp->bclgjp', cb[..., None] * seg, X)
    a_end = A[:, :, -1]
    states = jnp.einsum('bclgn,bclgjp->bcgjpn', Bc, X * jnp.exp(a_end[:, :, None] - A)[..., None])

    def step(s, inp):
        st, dec = inp
        return s * dec[..., None, None] + st, s

    s_fin, s_prev = lax.scan(step, s0, (jnp.moveaxis(states, 1, 0), jnp.moveaxis(jnp.exp(a_end), 1, 0)))
    s_prev = jnp.moveaxis(s_prev, 0, 1)
    y_off = jnp.einsum('bclgn,bcgjpn->bclgjp', Cc, s_prev) * jnp.exp(A)[..., None]
    return (y_diag + y_off).reshape(b, l, M_GROUPS, hpg, M_HEADDIM), s_fin


def _ssd_mixer(h_ctx, h_lat, w_in, conv_w, conv_b, dt_bias, a_log, d_skip, norm_g, w_out, ctx_out):
    d_inner = w_out.shape[0]
    n_heads = d_inner // M_HEADDIM
    hpg = n_heads // M_GROUPS
    conv_dim = d_inner + 2 * M_GROUPS * M_DSTATE

    def project(h):
        b, l, _ = h.shape
        zxbcdt = _proj(h, w_in)
        z = zxbcdt[..., :d_inner]
        xbc = jax.nn.silu(_dwconv1d(zxbcdt[..., d_inner:d_inner + conv_dim], conv_w) + conv_b)
        dt = zxbcdt[..., d_inner + conv_dim:].reshape(b, l, 2, M_GROUPS, hpg)
        xs = xbc[..., :d_inner].reshape(b, l, M_GROUPS, hpg, M_HEADDIM)
        bm = xbc[..., d_inner:d_inner + M_GROUPS * M_DSTATE].reshape(b, l, M_GROUPS, M_DSTATE)
        cm = xbc[..., d_inner + M_GROUPS * M_DSTATE:].reshape(b, l, M_GROUPS, M_DSTATE)
        return z, xs, bm, cm, dt

    a = -jnp.exp(a_log).reshape(2, M_GROUPS, hpg)
    dtb = dt_bias.reshape(2, M_GROUPS, hpg)

    def scan_dir(dd, xs, bm, cm, dt_raw, s0):
        dt = jax.nn.softplus(dt_raw[:, :, dd] + dtb[dd])
        return _ssd_chunked(xs * dt[..., None], dt * a[dd], bm, cm, s0)

    def flip(t):
        return jnp.flip(t, axis=1)

    def readout(y, xs, z):
        b, l = z.shape[:2]
        y = (y + d_skip.reshape(M_GROUPS, hpg)[..., None] * xs).reshape(b, l, d_inner)
        y = _rms_norm((y * jax.nn.silu(z)).reshape(b, l, M_GROUPS, -1)).reshape(b, l, d_inner) * norm_g
        return _proj(y, w_out)

    zc, xc, bc, cc, dtc = project(h_ctx)
    zl, xl, bl, cl, dtl = project(h_lat)
    s0 = jnp.zeros((h_lat.shape[0], M_GROUPS, hpg, M_HEADDIM, M_DSTATE), F32)
    yc_f, s_f = scan_dir(0, xc, bc, cc, dtc, s0)
    yl_f, _ = scan_dir(0, xl, bl, cl, dtl, s_f)
    yc_b, s_b = scan_dir(1, flip(xc), flip(bc), flip(cc), flip(dtc), s0)
    yl_b, _ = scan_dir(1, flip(xl), flip(bl), flip(cl), flip(dtl), s_b)
    y_lat = readout(yl_f + flip(yl_b), xl, zl)
    y_ctx = readout(yc_f + flip(yc_b), xc, zc) if ctx_out else None
    return y_ctx, y_lat


def kernel(x, c, ctx, c_ctx, mod_w, mod_b, ln_g, ln_b, conv_w_in, conv_w_dw, conv_ln_g, conv_ln_b,
           conv_w_out, hg_w_q, hg_w_f, hg_w_i, hg_w_g, hg_norm_g, hg_w_o, hg_lb_logits, m_w_in, m_conv_w,
           m_conv_b, m_dt_bias, m_A_log, m_D, m_norm_g, m_w_out, peer_w_q, peer_keys, peer_u, peer_v):
    depth = mod_w.shape[0]
    batch, _, d = x.shape
    dn_alpha = (2 * depth) ** 0.25
    reads_ctx = (False, True, True)
    last_ctx = max([i for i in range(depth) if reads_ctx[i % N_MIXERS]], default=-1)
    lb_all = jnp.cumsum(jax.nn.softmax(hg_lb_logits, axis=1), axis=1)
    lb_all = lb_all - lb_all[:, :1]

    n_mod = batch + 1
    n_mod_pad = -(-n_mod // SUBLANES) * SUBLANES
    cond = jnp.concatenate([jax.nn.silu(c), jax.nn.silu(c_ctx)[None, :],
                            jnp.zeros((n_mod_pad - n_mod, d), F32)], axis=0)

    for i in range(depth):
        kind = i % N_MIXERS
        j = i // N_MIXERS
        run_ctx = i <= last_ctx
        upd_ctx = i < last_ctx
        mod = _mm(cond.astype(BF16), mod_w[i], tn_cap=1024)[:n_mod] + mod_b[i]
        m_lat = jnp.split(mod[:batch, None, :], 6, axis=-1)
        m_ctx = jnp.split(mod[batch], 6, axis=-1)
        h_lat = x * (1.0 + m_lat[1]) + m_lat[0]
        h_ctx = ctx * (1.0 + m_ctx[1]) + m_ctx[0] if run_ctx else None
        if kind == 0:
            cw = (conv_w_in[j], conv_w_dw[j], conv_ln_g[j], conv_ln_b[j], conv_w_out[j])
            y_lat = _conv_mixer(h_lat, *cw, on_grid=True)
            y_ctx = _conv_mixer(h_ctx, *cw, on_grid=False) if upd_ctx else None
        elif kind == 1:
            y_ctx, y_lat = _hgrn_mixer(h_ctx, h_lat, hg_w_q[j], hg_w_f[j], hg_w_i[j], hg_w_g[j], hg_norm_g[j],
                                       hg_w_o[j], lb_all[:, i], upd_ctx)
        else:
            y_ctx, y_lat = _ssd_mixer(h_ctx, h_lat, m_w_in[j], m_conv_w[j], m_conv_b[j], m_dt_bias[j],
                                      m_A_log[j], m_D[j], m_norm_g[j], m_w_out[j], upd_ctx)
        pu = peer_u[i].astype(BF16)
        pv = peer_v[i].astype(BF16)
        pw = (peer_w_q[i], peer_keys[i], pu, pv)
        x = _layer_norm(dn_alpha * x + m_lat[2] * y_lat, ln_g[i, 0], ln_b[i, 0])
        x = _layer_norm(dn_alpha * x + m_lat[5] * _peer(x * (1.0 + m_lat[4]) + m_lat[3], *pw),
                        ln_g[i, 1], ln_b[i, 1])
        if upd_ctx:
            ctx = _layer_norm(dn_alpha * ctx + m_ctx[2] * y_ctx, ln_g[i, 0], ln_b[i, 0])
            ctx = _layer_norm(dn_alpha * ctx + m_ctx[5] * _peer(ctx * (1.0 + m_ctx[4]) + m_ctx[3], *pw),
                              ln_g[i, 1], ln_b[i, 1])
    return x
```

```python
import functools
import math

import jax
import jax.numpy as jnp
import numpy as np
from jax import lax
from jax.experimental import pallas as pl
from jax.experimental.pallas import tpu as pltpu

F32 = jnp.float32
BF16 = jnp.bfloat16

GRID_W = 64
N_MIXERS = 3
LN_EPS = 1e-5
HG_DK = 128
HG_CHUNK = 64
M_HEADDIM = 64
M_GROUPS = 8
M_DSTATE = 128
M_CHUNK = 64
PEER_TOPK = 16

LANES = 128
SUBLANES = 8
VMEM_LIMIT_BYTES = 56 * 1024 * 1024


def _pick_tile(n, cap):
    if n <= cap:
        return n
    best = None
    for t in range(LANES, cap + 1, LANES):
        if n % t == 0:
            best = t
    assert best is not None, (n, cap)
    return best


def _mm_kernel(a_ref, w_ref, o_ref):
    a = a_ref[...].astype(BF16)
    w = w_ref[...].astype(BF16)
    o_ref[...] = jnp.dot(a, w, preferred_element_type=F32).astype(o_ref.dtype)


def _mm(a, w, out_dtype=F32, tm_cap=512, tn_cap=512):
    m, k = a.shape
    k2, n = w.shape
    assert k == k2
    tm = _pick_tile(m, tm_cap) if m % SUBLANES == 0 and m > tm_cap else m
    tn = _pick_tile(n, tn_cap)
    return pl.pallas_call(
        _mm_kernel,
        grid=(m // tm, n // tn),
        in_specs=[pl.BlockSpec((tm, k), lambda i, j: (i, 0)),
                  pl.BlockSpec((k, tn), lambda i, j: (0, j))],
        out_specs=pl.BlockSpec((tm, tn), lambda i, j: (i, j)),
        out_shape=jax.ShapeDtypeStruct((m, n), out_dtype),
        compiler_params=pltpu.CompilerParams(
            dimension_semantics=("parallel", "arbitrary"), vmem_limit_bytes=VMEM_LIMIT_BYTES),
        name="proj_mm",
    )(a, w)


def _proj(h, w):
    b, l, k = h.shape
    return _mm(h.reshape(b * l, k).astype(BF16), w.astype(BF16)).reshape(b, l, w.shape[1])


def _topk_desc(x, k):
    vals = []
    for a in range(k):
        m = jnp.max(x, axis=0, keepdims=True)
        vals.append(m)
        if a + 1 < k:
            x = jnp.where(x == m, -jnp.inf, x)
    return vals


def _peer_route_kernel(h_ref, wq_ref, kbd_ref, s2_ref, e2_ref, thr_ref, e1_ref, *, n_heads, n_keys):
    q = jnp.dot(h_ref[...], wq_ref[...], preferred_element_type=F32)
    st = lax.dot_general(kbd_ref[...], q.astype(BF16), (((1,), (1,)), ((), ())),
                         preferred_element_type=F32)
    for h in range(n_heads):
        s1 = st[(2 * h) * n_keys:(2 * h + 1) * n_keys, :]
        s2 = st[(2 * h + 1) * n_keys:(2 * h + 2) * n_keys, :]
        sv1 = _topk_desc(s1, PEER_TOPK)
        sv2 = _topk_desc(s2, PEER_TOPK)
        sv2_all = jnp.concatenate(sv2, axis=0)
        cand = jnp.concatenate([sv1[a] + sv2_all for a in range(PEER_TOPK)], axis=0)
        th = _topk_desc(cand, PEER_TOPK)[-1]
        top = sv1[0] + sv2[0]
        z = jnp.sum(jnp.where(cand >= th, jnp.exp(cand - top), 0.0), axis=0, keepdims=True)
        thr = jnp.full(s1.shape, jnp.inf, F32)
        for b in range(PEER_TOPK):
            thr = jnp.where(s1 + sv2[b] >= th, sv2[b], thr)
        s2_ref[h] = s2
        e2_ref[h] = jnp.exp(s2 - sv2[0])
        thr_ref[h] = thr
        e1_ref[h] = jnp.exp(s1 - sv1[0]) / z


def _peer_route(hm, wq, kbd, n_heads, n_keys, tm):
    t, d = hm.shape
    out_sd = jax.ShapeDtypeStruct((n_heads, n_keys, t), F32)
    out_spec = pl.BlockSpec((n_heads, n_keys, tm), lambda i: (0, 0, i))
    return pl.pallas_call(
        functools.partial(_peer_route_kernel, n_heads=n_heads, n_keys=n_keys),
        grid=(t // tm,),
        in_specs=[pl.BlockSpec((tm, d), lambda i: (i, 0)),
                  pl.BlockSpec(wq.shape, lambda i: (0, 0)),
                  pl.BlockSpec(kbd.shape, lambda i: (0, 0))],
        out_specs=[out_spec] * 4,
        out_shape=[out_sd] * 4,
        compiler_params=pltpu.CompilerParams(
            dimension_semantics=("parallel",), vmem_limit_bytes=VMEM_LIMIT_BYTES),
        name="peer_route",
    )(hm, wq, kbd)


def _gelu_exact(x):
    return 0.5 * x * (1.0 + lax.erf(x * (1.0 / math.sqrt(2.0))))


def _peer_ffn_kernel(x_ref, u_ref, v_ref, s2_ref, e2_ref, thr_ref, e1_ref, o_ref, ga_ref, *,
                     n_heads, n_keys, rows_per_step):
    @pl.when(pl.program_id(1) == 0)
    def _():
        o_ref[...] = jnp.zeros_like(o_ref)

    x = x_ref[...]
    for ii in range(rows_per_step):
        lo, hi = ii * n_keys, (ii + 1) * n_keys
        a = lax.dot_general(u_ref[lo:hi, :], x, (((1,), (1,)), ((), ())),
                            preferred_element_type=F32)
        gate = jnp.zeros(a.shape, F32)
        for h in range(n_heads):
            thr = thr_ref[h, ii:ii + 1, :]
            w = e1_ref[h, ii:ii + 1, :]
            gate = gate + jnp.where(s2_ref[h] >= thr, e2_ref[h], 0.0) * w
        ga_ref[lo:hi, :] = (gate * _gelu_exact(a)).astype(BF16)
    o_ref[...] += lax.dot_general(ga_ref[...], v_ref[...], (((0,), (0,)), ((), ())),
                                  preferred_element_type=F32)


def _peer_ffn_dense(hm, u, v, s2, e2, thr, e1, n_heads, n_keys, tm):
    t, d = hm.shape
    n_exp = u.shape[0]
    rows_per_step = SUBLANES
    ec = rows_per_step * n_keys
    tok_spec = pl.BlockSpec((n_heads, n_keys, tm), lambda i, e: (0, 0, i))
    row_spec = pl.BlockSpec((n_heads, rows_per_step, tm), lambda i, e: (0, e, i))
    return pl.pallas_call(
        functools.partial(_peer_ffn_kernel, n_heads=n_heads, n_keys=n_keys, rows_per_step=rows_per_step),
        grid=(t // tm, n_exp // ec),
        in_specs=[pl.BlockSpec((tm, d), lambda i, e: (i, 0)),
                  pl.BlockSpec((ec, d), lambda i, e: (e, 0)),
                  pl.BlockSpec((ec, d), lambda i, e: (e, 0)),
                  tok_spec, tok_spec, row_spec, row_spec],
        out_specs=pl.BlockSpec((tm, d), lambda i, e: (i, 0)),
        out_shape=jax.ShapeDtypeStruct((t, d), F32),
        scratch_shapes=[pltpu.VMEM((ec, tm), BF16)],
        compiler_params=pltpu.CompilerParams(
            dimension_semantics=("parallel", "arbitrary"), vmem_limit_bytes=VMEM_LIMIT_BYTES),
        name="peer_ffn",
    )(hm, u, v, s2, e2, thr, e1)


def _peer(h, w_q, keys, u, v):
    b, l, d = h.shape
    n_heads, _, n_keys, half = keys.shape
    hm = h.reshape(b * l, d).astype(BF16)
    eye = jnp.eye(2 * n_heads, dtype=F32)
    kflat = keys.reshape(2 * n_heads, n_keys, half)
    kbd = (eye[:, None, :, None] * kflat[:, :, None, :]).reshape(2 * n_heads * n_keys, 2 * n_heads * half)
    tm_route = _pick_tile(b * l, 256)
    s2, e2, thr, e1 = _peer_route(hm, w_q.astype(BF16), kbd.astype(BF16), n_heads, n_keys, tm_route)
    tm = _pick_tile(b * l, 512)
    out = _peer_ffn_dense(hm, u, v, s2, e2, thr, e1, n_heads, n_keys, tm)
    return out.reshape(b, l, d)


def _layer_norm(x, g, b):
    mu = jnp.mean(x, -1, keepdims=True)
    var = jnp.mean(jnp.square(x - mu), -1, keepdims=True)
    return (x - mu) * lax.rsqrt(var + LN_EPS) * g + b


def _rms_norm(x, eps=1e-6):
    return x * lax.rsqrt(jnp.mean(jnp.square(x), -1, keepdims=True) + eps)


def _dwconv1d(u, w):
    k = w.shape[0]
    return lax.conv_general_dilated(u, w[:, None, :], (1,), [(k // 2, k // 2)],
                                    dimension_numbers=('NWC', 'WIO', 'NWC'),
                                    feature_group_count=u.shape[-1])


def _axial_dwconv(u, w):
    b, n, ch = u.shape
    split = ch // 2
    rows = n // GRID_W
    uh = u[..., :split].reshape(b * rows, GRID_W, split)
    yh = _dwconv1d(uh, w[:, :split]).reshape(b, n, split)
    cv = ch - split
    uv = u[..., split:].reshape(b, rows, GRID_W, cv).transpose(0, 2, 1, 3).reshape(b * GRID_W, rows, cv)
    yv = _dwconv1d(uv, w[:, split:]).reshape(b, GRID_W, rows, cv).transpose(0, 2, 1, 3).reshape(b, n, cv)
    return jnp.concatenate([yh, yv], axis=-1)


def _conv_mixer(h, w_in, w_dw, ln_g, ln_b, w_out, on_grid):
    d = h.shape[-1]
    a = _proj(h, w_in)
    u = a[..., :d] * jax.nn.sigmoid(a[..., d:])
    u = _axial_dwconv(u, w_dw) if on_grid else _dwconv1d(u, w_dw)
    u = jax.nn.silu(_layer_norm(u, ln_g, ln_b))
    return _proj(u, w_out)


def _hgrn_consts(chunk):
    n = chunk
    levels = int(math.log2(n))
    assert 1 << levels == n
    idx = np.arange(n)
    mats = np.zeros((2, n * (1 + 2 * levels) + SUBLANES, n), np.float32)
    pair = np.zeros((2, n * (levels + 1), n), np.float32)
    jj = idx[None, :]
    tt = idx[:, None]
    for d in range(2):
        mats[d, :n] = (jj <= tt) if d == 0 else (jj >= tt)
        for l in range(levels):
            m = 1 << l
            blk = idx // (2 * m)
            far = (idx // m) % 2 == 1
            mid = (blk * 2 * m + m)[:, None]
            same = blk[:, None] == blk[None, :]
            if d == 0:
                dq = far[:, None] & (jj >= mid) & (jj <= tt)
                dk = (~far)[:, None] & (jj > tt) & (jj <= mid - 1)
                pm = far[:, None] & (~far)[None, :] & same
            else:
                dq = (~far)[:, None] & (jj >= tt) & (jj <= mid - 1)
                dk = far[:, None] & (jj >= mid) & (jj < tt)
                pm = (~far)[:, None] & far[None, :] & same
            mats[d, n * (1 + l):n * (2 + l)] = dq
            mats[d, n * (1 + levels + l):n * (2 + levels + l)] = dk
            pair[d, n * l:n * (l + 1)] = pm
        mats[d, n * (1 + 2 * levels):] = 1.0
        pair[d, n * levels:] = np.eye(n)
    return mats, pair, levels


def _hgrn_scan_kernel(q_ref, v_ref, z_ref, lb_ref, m_ref, pm_ref, s0_ref, o_ref, sfin_ref, st_ref, *,
                      chunk, levels, n_chunks):
    d = pl.program_id(0)

    @pl.when(pl.program_id(3) == 0)
    def _():
        st_ref[...] = s0_ref[...]

    lb = lb_ref[...]
    log_lb = jnp.log(lb)
    log_1mlb = jnp.log1p(-lb)
    n = chunk

    def body(jj, carry):
        j = jj + d * (n_chunks - 1 - 2 * jj)
        rows = pl.ds(pl.multiple_of(j * n, n), n)
        qc = q_ref[rows, :]
        qc = qc * jax.nn.sigmoid(qc)
        vc = v_ref[rows, :].astype(BF16)
        zc = z_ref[rows, :]
        ls = jnp.minimum(zc, 0.0) - jnp.log1p(jnp.exp(-jnp.abs(zc)))
        b2 = log_1mlb + ls
        g = jnp.maximum(log_lb, b2) + jnp.log1p(jnp.exp(-jnp.abs(log_lb - b2)))
        kc = (1.0 - lb) * jax.nn.sigmoid(-zc)
        g_hi = g.astype(BF16)
        g_lo = (g - g_hi.astype(F32)).astype(BF16)
        e2 = jnp.dot(m_ref[...], jnp.concatenate([g_hi, g_lo], axis=1), preferred_element_type=F32)
        e = e2[:, :HG_DK] + e2[:, HG_DK:]
        bc = e[:n]
        tot = e[n * (1 + 2 * levels):n * (1 + 2 * levels) + 1]
        nt = (((1,), (1,)), ((), ()))
        att = lax.dot_general(qc.astype(BF16), kc.astype(BF16), nt,
                              preferred_element_type=F32) * pm_ref[n * levels:n * (levels + 1), :]
        for l in range(levels):
            ql = (qc * jnp.exp(e[n * (1 + l):n * (2 + l)])).astype(BF16)
            kl = (kc * jnp.exp(e[n * (1 + levels + l):n * (2 + levels + l)])).astype(BF16)
            att = att + lax.dot_general(ql, kl, nt, preferred_element_type=F32) * pm_ref[n * l:n * (l + 1), :]
        st = st_ref[...]
        qbar = (qc * jnp.exp(bc)).astype(BF16)
        o = jnp.dot(att.astype(BF16), vc, preferred_element_type=F32)
        o = o + lax.dot_general(qbar, st.astype(BF16), nt, preferred_element_type=F32)
        o_ref[rows, :] = o
        khat = (kc * jnp.exp(tot - bc)).astype(BF16)
        st_ref[...] = st * jnp.exp(tot) + lax.dot_general(vc, khat, (((0,), (0,)), ((), ())),
                                                          preferred_element_type=F32)
        return carry

    lax.fori_loop(0, n_chunks, body, 0, unroll=2)
    sfin_ref[...] = st_ref[...]


def _hgrn_scan(p, lb, s0, n_heads, rows_cap=512):
    b, l, _ = p.shape
    d_model = n_heads * HG_DK
    mats, pair, levels = _hgrn_consts(HG_CHUNK)
    rows = min(l, rows_cap)
    assert l % rows == 0 and rows % HG_CHUNK == 0
    nblk = l // rows

    def blk(dd, c):
        return c + dd * (nblk - 1 - 2 * c)

    def col(off):
        return pl.BlockSpec((None, rows, HG_DK), lambda dd, bb, hh, c: (bb, blk(dd, c), off + hh))

    zspec = pl.BlockSpec((None, rows, HG_DK), lambda dd, bb, hh, c: (bb, blk(dd, c), (2 + dd) * n_heads + hh))
    st_spec = pl.BlockSpec((None, None, None, HG_DK, HG_DK), lambda dd, bb, hh, c: (dd, bb, hh, 0, 0))
    return pl.pallas_call(
        functools.partial(_hgrn_scan_kernel, chunk=HG_CHUNK, levels=levels, n_chunks=rows // HG_CHUNK),
        grid=(2, b, n_heads, nblk),
        in_specs=[col(0), col(n_heads), zspec,
                  pl.BlockSpec((None, 1, HG_DK), lambda dd, bb, hh, c: (dd, 0, hh)),
                  pl.BlockSpec((None,) + mats.shape[1:], lambda dd, bb, hh, c: (dd, 0, 0)),
                  pl.BlockSpec((None,) + pair.shape[1:], lambda dd, bb, hh, c: (dd, 0, 0)),
                  st_spec],
        out_specs=[pl.BlockSpec((None, None, rows, HG_DK), lambda dd, bb, hh, c: (dd, bb, blk(dd, c), hh)),
                   st_spec],
        out_shape=[jax.ShapeDtypeStruct((2, b, l, d_model), F32),
                   jax.ShapeDtypeStruct(s0.shape, F32)],
        scratch_shapes=[pltpu.VMEM((HG_DK, HG_DK), F32)],
        compiler_params=pltpu.CompilerParams(
            dimension_semantics=("parallel", "parallel", "parallel", "arbitrary"),
            vmem_limit_bytes=VMEM_LIMIT_BYTES),
        name="hgrn_scan",
    )(p, p, p, lb.reshape(2, 1, d_model), jnp.asarray(mats, BF16), jnp.asarray(pair, F32), s0)


def _hgrn_mixer(h_ctx, h_lat, w_q, w_f, w_i, w_g, norm_g, w_o, lb, ctx_out):
    d = h_lat.shape[-1]
    n_heads = d // HG_DK
    w_cat = jnp.concatenate([w_q, w_i, w_f[0], w_f[1], w_g], axis=1)

    def readout(o2, p):
        b, l, _ = p.shape
        o = (o2[0] + o2[1]).reshape(b, l, n_heads, HG_DK)
        o = _rms_norm(o).reshape(b, l, d) * norm_g
        return _proj(o * jax.nn.silu(p[..., 4 * d:]), w_o)

    p_ctx = _proj(h_ctx, w_cat)
    p_lat = _proj(h_lat, w_cat)
    s0 = jnp.zeros((2, h_lat.shape[0], n_heads, HG_DK, HG_DK), F32)
    o_ctx, s_ctx = _hgrn_scan(p_ctx, lb, s0, n_heads)
    o_lat, _ = _hgrn_scan(p_lat, lb, s_ctx, n_heads)
    y_lat = readout(o_lat, p_lat)
    y_ctx = readout(o_ctx, p_ctx) if ctx_out else None
    return y_ctx, y_lat


def _split_bf16(x):
    hi = x.astype(BF16)
    return hi, (x - hi.astype(F32)).astype(BF16)


def _dwconv_silu_kernel(x_ref, w_ref, b_ref, o_ref, pad_ref, *, taps, halo):
    l = x_ref.shape[0]
    zeros = jnp.zeros((halo, pad_ref.shape[1]), F32)
    pad_ref[0:halo, :] = zeros
    pad_ref[halo + l:2 * halo + l, :] = zeros
    pad_ref[halo:halo + l, :] = x_ref[...]
    acc = jnp.zeros(o_ref.shape, F32) + b_ref[...]
    for k in range(taps):
        start = halo + k - taps // 2
        acc = acc + pad_ref[start:start + l, :] * w_ref[k:k + 1, :]
    o_ref[...] = acc * jax.nn.sigmoid(acc)


def _dwconv_silu(p, col0, n_cols, w, bias):
    b, l, _ = p.shape
    taps = w.shape[0]
    halo = SUBLANES
    assert taps // 2 <= halo and col0 % LANES == 0
    ct = _pick_tile(n_cols, 512)
    assert col0 % ct == 0
    return pl.pallas_call(
        functools.partial(_dwconv_silu_kernel, taps=taps, halo=halo),
        grid=(b, n_cols // ct),
        in_specs=[pl.BlockSpec((None, l, ct), lambda bb, j: (bb, 0, col0 // ct + j)),
                  pl.BlockSpec((taps, ct), lambda bb, j: (0, j)),
                  pl.BlockSpec((1, ct), lambda bb, j: (0, j))],
        out_specs=pl.BlockSpec((None, l, ct), lambda bb, j: (bb, 0, j)),
        out_shape=jax.ShapeDtypeStruct((b, l, n_cols), F32),
        scratch_shapes=[pltpu.VMEM((l + 2 * halo, ct), F32)],
        compiler_params=pltpu.CompilerParams(
            dimension_semantics=("parallel", "parallel"), vmem_limit_bytes=VMEM_LIMIT_BYTES),
        name="dwconv_silu",
    )(p, w, bias.reshape(1, n_cols))


def _ssd_consts(chunk, n_heads, hpg, headdim):
    n = chunk
    width = hpg * headdim
    idx = np.arange(n)
    tri = np.stack([idx[None, :] <= idx[:, None], idx[None, :] >= idx[:, None]]).astype(np.float32)
    pos = np.arange(width) % headdim
    head = np.arange(width) // headdim
    assert headdim == n
    eye_t = (idx[:, None] == pos[None, :]).astype(np.float32)
    keep = np.stack([idx[:, None] >= pos[None, :], idx[:, None] <= pos[None, :]]).astype(np.float32)
    blockmask = (head[:, None] == head[None, :]).astype(np.float32)
    n_groups = n_heads // hpg
    expand = np.zeros((2, n_groups, 2 * n_heads, width), np.float32)
    for d in range(2):
        for g in range(n_groups):
            expand[d, g, d * n_heads + g * hpg + head, np.arange(width)] = 1.0
    return tri, eye_t, keep, blockmask, expand


def _ssd_scan_kernel(x_ref, b_ref, c_ref, dt_ref, dtb_ref, a_ref, ex_ref, tri_ref, eye_ref, keep_ref,
                     bmask_ref, s0_ref, y_ref, sfin_ref, st_ref, *, chunk, n_chunks, hpg):
    d = pl.program_id(0)

    @pl.when(pl.program_id(3) == 0)
    def _():
        st_ref[...] = s0_ref[...]

    n = chunk
    nt = (((1,), (1,)), ((), ()))
    tn = (((0,), (0,)), ((), ()))
    ones_rows = jnp.ones((SUBLANES, n), BF16)

    def body(jj, carry):
        j = jj + d * (n_chunks - 1 - 2 * jj)
        rows = pl.ds(pl.multiple_of(j * n, n), n)
        x = x_ref[rows, :]
        bm = b_ref[rows, :].astype(BF16)
        cm = c_ref[rows, :].astype(BF16)
        raw = dt_ref[rows, :] + dtb_ref[...]
        dt = jnp.maximum(raw, 0.0) + jnp.log1p(jnp.exp(-jnp.abs(raw)))
        da = dt * a_ref[...]
        da_hi, da_lo = _split_bf16(da)
        acum = jnp.dot(tri_ref[...], jnp.concatenate([da_hi, da_lo], axis=1), preferred_element_type=F32)
        acum = acum[:, :da.shape[1]] + acum[:, da.shape[1]:]
        dt_hi, dt_lo = _split_bf16(dt)
        ac_hi, ac_lo = _split_bf16(acum)
        ex = jnp.dot(jnp.concatenate([dt_hi, dt_lo, ac_hi, ac_lo], axis=0), ex_ref[...],
                     preferred_element_type=F32)
        dt_w = ex[0:n] + ex[n:2 * n]
        a_col = ex[2 * n:3 * n] + ex[3 * n:4 * n]
        d_hi, d_lo = _split_bf16(a_col * eye_ref[...])
        a_row = (jnp.dot(ones_rows, d_hi, preferred_element_type=F32)
                 + jnp.dot(ones_rows, d_lo, preferred_element_type=F32))[0:1]
        t_hi, t_lo = _split_bf16(da)
        a_tot = jnp.dot(jnp.concatenate([ones_rows, ones_rows], axis=1),
                        jnp.concatenate([t_hi, t_lo], axis=0), preferred_element_type=F32)
        tw_hi, tw_lo = _split_bf16(a_tot)
        a_end = (jnp.dot(tw_hi, ex_ref[...], preferred_element_type=F32)
                 + jnp.dot(tw_lo, ex_ref[...], preferred_element_type=F32))[0:1]
        keep = keep_ref[...] > 0.5
        seg = jnp.exp(jnp.where(keep, a_col - a_row, -jnp.inf))
        cb = lax.dot_general(cm, jnp.concatenate([bm] * hpg, axis=0), nt, preferred_element_type=F32)
        xdt = x * dt_w
        xbd = (jnp.concatenate([xdt] * hpg, axis=0) * bmask_ref[...]).astype(BF16)
        y = jnp.dot((cb * seg).astype(BF16), xbd, preferred_element_type=F32)
        st = st_ref[...]
        y = y + jnp.dot(cm, st.astype(BF16), preferred_element_type=F32) * jnp.exp(a_col)
        y_ref[rows, :] = y
        xdec = (xdt * jnp.exp(a_end - a_col)).astype(BF16)
        st_ref[...] = st * jnp.exp(a_end) + lax.dot_general(bm, xdec, tn, preferred_element_type=F32)
        return carry

    lax.fori_loop(0, n_chunks, body, 0)
    sfin_ref[...] = st_ref[...]


def _ssd_scan(p, xbc, dt_bias, a_neg, s0, d_inner, rows_cap=512):
    b, l, _ = p.shape
    n_heads = d_inner // M_HEADDIM
    hpg = n_heads // M_GROUPS
    width = hpg * M_HEADDIM
    assert 2 * n_heads == LANES and width % LANES == 0
    tri, eye_t, keep, blockmask, expand = _ssd_consts(M_CHUNK, n_heads, hpg, M_HEADDIM)
    rows = min(l, rows_cap)
    assert l % rows == 0 and rows % M_CHUNK == 0
    nblk = l // rows
    dt_block = (p.shape[2] - 2 * n_heads) // LANES
    b_block0 = d_inner // M_DSTATE
    c_block0 = b_block0 + M_GROUPS

    def blk(dd, c):
        return c + dd * (nblk - 1 - 2 * c)

    const2 = lambda shape: pl.BlockSpec(shape, lambda dd, bb, gg, c: (0, 0))
    per_dir = lambda shape: pl.BlockSpec((None,) + shape, lambda dd, bb, gg, c: (dd, 0, 0))
    st_spec = pl.BlockSpec((None, None, None, M_DSTATE, width), lambda dd, bb, gg, c: (dd, bb, gg, 0, 0))
    return pl.pallas_call(
        functools.partial(_ssd_scan_kernel, chunk=M_CHUNK, n_chunks=rows // M_CHUNK, hpg=hpg),
        grid=(2, b, M_GROUPS, nblk),
        in_specs=[pl.BlockSpec((None, rows, width), lambda dd, bb, gg, c: (bb, blk(dd, c), gg)),
                  pl.BlockSpec((None, rows, M_DSTATE), lambda dd, bb, gg, c: (bb, blk(dd, c), b_block0 + gg)),
                  pl.BlockSpec((None, rows, M_DSTATE), lambda dd, bb, gg, c: (bb, blk(dd, c), c_block0 + gg)),
                  pl.BlockSpec((None, rows, LANES), lambda dd, bb, gg, c: (bb, blk(dd, c), dt_block)),
                  const2((1, LANES)), const2((1, LANES)),
                  pl.BlockSpec((None, None, LANES, width), lambda dd, bb, gg, c: (dd, gg, 0, 0)),
                  per_dir((M_CHUNK, M_CHUNK)), const2((M_CHUNK, width)), per_dir((M_CHUNK, width)),
                  const2((width, width)), st_spec],
        out_specs=[pl.BlockSpec((None, None, rows, width), lambda dd, bb, gg, c: (dd, bb, blk(dd, c), gg)),
                   st_spec],
        out_shape=[jax.ShapeDtypeStruct((2, b, l, d_inner), F32),
                   jax.ShapeDtypeStruct(s0.shape, F32)],
        scratch_shapes=[pltpu.VMEM((M_DSTATE, width), F32)],
        compiler_params=pltpu.CompilerParams(
            dimension_semantics=("parallel", "parallel", "parallel", "arbitrary"),
            vmem_limit_bytes=VMEM_LIMIT_BYTES),
        name="ssd_scan",
    )(xbc, xbc, xbc, p, dt_bias.reshape(1, LANES), a_neg.reshape(1, LANES),
      jnp.asarray(expand, BF16), jnp.asarray(tri, BF16), jnp.asarray(eye_t, F32), jnp.asarray(keep, F32),
      jnp.asarray(blockmask, F32), s0)


def _ssd_mixer(h_ctx, h_lat, w_in, conv_w, conv_b, dt_bias, a_log, d_skip, norm_g, w_out, ctx_out):
    d_inner = w_out.shape[0]
    n_heads = d_inner // M_HEADDIM
    hpg = n_heads // M_GROUPS
    conv_dim = d_inner + 2 * M_GROUPS * M_DSTATE
    a_neg = -jnp.exp(a_log)

    def project(h):
        p = _proj(h, w_in)
        return p, _dwconv_silu(p, d_inner, conv_dim, conv_w, conv_b)

    def readout(y2, p, xbc):
        b, l, _ = p.shape
        xs = xbc[..., :d_inner]
        skip = jnp.repeat(d_skip, M_HEADDIM)
        y = (y2[0] + y2[1] + skip * xs) * jax.nn.silu(p[..., :d_inner])
        y = _rms_norm(y.reshape(b, l, M_GROUPS, -1)).reshape(b, l, d_inner) * norm_g
        return _proj(y, w_out)

    p_ctx, xbc_ctx = project(h_ctx)
    p_lat, xbc_lat = project(h_lat)
    s0 = jnp.zeros((2, h_lat.shape[0], M_GROUPS, M_DSTATE, hpg * M_HEADDIM), F32)
    y_ctx2, s_ctx = _ssd_scan(p_ctx, xbc_ctx, dt_bias, a_neg, s0, d_inner)
    y_lat2, _ = _ssd_scan(p_lat, xbc_lat, dt_bias, a_neg, s_ctx, d_inner)
    y_lat = readout(y_lat2, p_lat, xbc_lat)
    y_ctx = readout(y_ctx2, p_ctx, xbc_ctx) if ctx_out else None
    return y_ctx, y_lat


def kernel(x, c, ctx, c_ctx, mod_w, mod_b, ln_g, ln_b, conv_w_in, conv_w_dw, conv_ln_g, conv_ln_b,
           conv_w_out, hg_w_q, hg_w_f, hg_w_i, hg_w_g, hg_norm_g, hg_w_o, hg_lb_logits, m_w_in, m_conv_w,
           m_conv_b, m_dt_bias, m_A_log, m_D, m_norm_g, m_w_out, peer_w_q, peer_keys, peer_u, peer_v):
    depth = mod_w.shape[0]
    batch, _, d = x.shape
    dn_alpha = (2 * depth) ** 0.25
    reads_ctx = (False, True, True)
    last_ctx = max([i for i in range(depth) if reads_ctx[i % N_MIXERS]], default=-1)
    lb_all = jnp.cumsum(jax.nn.softmax(hg_lb_logits, axis=1), axis=1)
    lb_all = lb_all - lb_all[:, :1]

    n_mod = batch + 1
    n_mod_pad = -(-n_mod // SUBLANES) * SUBLANES
    cond = jnp.concatenate([jax.nn.silu(c), jax.nn.silu(c_ctx)[None, :],
                            jnp.zeros((n_mod_pad - n_mod, d), F32)], axis=0)

    for i in range(depth):
        kind = i % N_MIXERS
        j = i // N_MIXERS
        run_ctx = i <= last_ctx
        upd_ctx = i < last_ctx
        mod = _mm(cond.astype(BF16), mod_w[i], tn_cap=1024)[:n_mod] + mod_b[i]
        m_lat = jnp.split(mod[:batch, None, :], 6, axis=-1)
        m_ctx = jnp.split(mod[batch], 6, axis=-1)
        h_lat = x * (1.0 + m_lat[1]) + m_lat[0]
        h_ctx = ctx * (1.0 + m_ctx[1]) + m_ctx[0] if run_ctx else None
        if kind == 0:
            cw = (conv_w_in[j], conv_w_dw[j], conv_ln_g[j], conv_ln_b[j], conv_w_out[j])
            y_lat = _conv_mixer(h_lat, *cw, on_grid=True)
            y_ctx = _conv_mixer(h_ctx, *cw, on_grid=False) if upd_ctx else None
        elif kind == 1:
            y_ctx, y_lat = _hgrn_mixer(h_ctx, h_lat, hg_w_q[j], hg_w_f[j], hg_w_i[j], hg_w_g[j], hg_norm_g[j],
                                       hg_w_o[j], lb_all[:, i], upd_ctx)
        else:
            y_ctx, y_lat = _ssd_mixer(h_ctx, h_lat, m_w_in[j], m_conv_w[j], m_conv_b[j], m_dt_bias[j],
                                      m_A_log[j], m_D[j], m_norm_g[j], m_w_out[j], upd_ctx)
        pu = peer_u[i].astype(BF16)
        pv = peer_v[i].astype(BF16)
        pw = (peer_w_q[i], peer_keys[i], pu, pv)
        x = _layer_norm(dn_alpha * x + m_lat[2] * y_lat, ln_g[i, 0], ln_b[i, 0])
        x = _layer_norm(dn_alpha * x + m_lat[5] * _peer(x * (1.0 + m_lat[4]) + m_lat[3], *pw),
                        ln_g[i, 1], ln_b[i, 1])
        if upd_ctx:
            ctx = _layer_norm(dn_alpha * ctx + m_ctx[2] * y_ctx, ln_g[i, 0], ln_b[i, 0])
            ctx = _layer_norm(dn_alpha * ctx + m_ctx[5] * _peer(ctx * (1.0 + m_ctx[4]) + m_ctx[3], *pw),
                              ln_g[i, 1], ln_b[i, 1])
    return x
```

```python
import functools
import math

import jax
import jax.numpy as jnp
import numpy as np
from jax import lax
from jax.experimental import pallas as pl
from jax.experimental.pallas import tpu as pltpu

F32 = jnp.float32
BF16 = jnp.bfloat16

GRID_W = 64
N_MIXERS = 3
LN_EPS = 1e-5
HG_DK = 128
HG_CHUNK = 64
HG_HEADS_PER_STEP = 4
M_HEADDIM = 64
M_GROUPS = 8
M_GROUPS_PER_STEP = 2
M_DSTATE = 128
M_CHUNK = 64
PEER_TOPK = 16

LANES = 128
SUBLANES = 8
VMEM_LIMIT_BYTES = 56 * 1024 * 1024


def _pick_tile(n, cap):
    if n <= cap:
        return n
    best = None
    for t in range(LANES, cap + 1, LANES):
        if n % t == 0:
            best = t
    assert best is not None, (n, cap)
    return best


def _mm_kernel(a_ref, w_ref, o_ref):
    a = a_ref[...].astype(BF16)
    w = w_ref[...].astype(BF16)
    o_ref[...] = jnp.dot(a, w, preferred_element_type=F32).astype(o_ref.dtype)


def _mm(a, w, out_dtype=F32, tm_cap=1024, tn_cap=1152):
    m, k = a.shape
    k2, n = w.shape
    assert k == k2
    tm = _pick_tile(m, tm_cap) if m % SUBLANES == 0 and m > tm_cap else m
    tn = _pick_tile(n, tn_cap)
    return pl.pallas_call(
        _mm_kernel,
        grid=(m // tm, n // tn),
        in_specs=[pl.BlockSpec((tm, k), lambda i, j: (i, 0)),
                  pl.BlockSpec((k, tn), lambda i, j: (0, j))],
        out_specs=pl.BlockSpec((tm, tn), lambda i, j: (i, j)),
        out_shape=jax.ShapeDtypeStruct((m, n), out_dtype),
        compiler_params=pltpu.CompilerParams(
            dimension_semantics=("parallel", "arbitrary"), vmem_limit_bytes=VMEM_LIMIT_BYTES),
        name="proj_mm",
    )(a, w)


def _proj(h, w):
    b, l, k = h.shape
    return _mm(h.reshape(b * l, k).astype(BF16), w.astype(BF16)).reshape(b, l, w.shape[1])


def _topk_desc(x, k):
    vals = []
    for a in range(k):
        m = jnp.max(x, axis=0, keepdims=True)
        vals.append(m)
        if a + 1 < k:
            x = jnp.where(x == m, -jnp.inf, x)
    return vals


def _peer_route_kernel(h_ref, wq_ref, kbd_ref, s2_ref, e2_ref, thr_ref, e1_ref, *, n_heads, n_keys):
    q = jnp.dot(h_ref[...], wq_ref[...], preferred_element_type=F32)
    st = lax.dot_general(kbd_ref[...], q.astype(BF16), (((1,), (1,)), ((), ())),
                         preferred_element_type=F32)
    for h in range(n_heads):
        s1 = st[(2 * h) * n_keys:(2 * h + 1) * n_keys, :]
        s2 = st[(2 * h + 1) * n_keys:(2 * h + 2) * n_keys, :]
        sv1 = _topk_desc(s1, PEER_TOPK)
        sv2 = _topk_desc(s2, PEER_TOPK)
        sv2_all = jnp.concatenate(sv2, axis=0)
        cand = jnp.concatenate([sv1[a] + sv2_all for a in range(PEER_TOPK)], axis=0)
        th = _topk_desc(cand, PEER_TOPK)[-1]
        top = sv1[0] + sv2[0]
        z = jnp.sum(jnp.where(cand >= th, jnp.exp(cand - top), 0.0), axis=0, keepdims=True)
        thr = jnp.full(s1.shape, jnp.inf, F32)
        for b in range(PEER_TOPK):
            thr = jnp.where(s1 + sv2[b] >= th, sv2[b], thr)
        s2_ref[h] = s2
        e2_ref[h] = jnp.exp(s2 - sv2[0])
        thr_ref[h] = thr
        e1_ref[h] = jnp.exp(s1 - sv1[0]) / z


def _peer_route(hm, wq, kbd, n_heads, n_keys, tm):
    t, d = hm.shape
    out_sd = jax.ShapeDtypeStruct((n_heads, n_keys, t), F32)
    out_spec = pl.BlockSpec((n_heads, n_keys, tm), lambda i: (0, 0, i))
    return pl.pallas_call(
        functools.partial(_peer_route_kernel, n_heads=n_heads, n_keys=n_keys),
        grid=(t // tm,),
        in_specs=[pl.BlockSpec((tm, d), lambda i: (i, 0)),
                  pl.BlockSpec(wq.shape, lambda i: (0, 0)),
                  pl.BlockSpec(kbd.shape, lambda i: (0, 0))],
        out_specs=[out_spec] * 4,
        out_shape=[out_sd] * 4,
        compiler_params=pltpu.CompilerParams(
            dimension_semantics=("parallel",), vmem_limit_bytes=VMEM_LIMIT_BYTES),
        name="peer_route",
    )(hm, wq, kbd)


def _gelu_exact(x):
    return 0.5 * x * (1.0 + lax.erf(x * (1.0 / math.sqrt(2.0))))


def _peer_ffn_kernel(x_ref, u_ref, v_ref, s2_ref, e2_ref, thr_ref, e1_ref, o_ref, ga0_ref, ga1_ref, a0_ref, a1_ref, *,
                     n_heads, n_keys, rows_per_step, n_chunks):
    s = pl.program_id(0)

    @pl.when(s == 0)
    def _():
        ga1_ref[...] = jnp.zeros_like(ga1_ref)

    @pl.when(jnp.maximum(s - 1, 0) % n_chunks == 0)
    def _():
        o_ref[...] = jnp.zeros_like(o_ref)

    d_cols = o_ref.shape[1] // rows_per_step

    def body(ga_w, ga_r):
        x = x_ref[...]

        a_refs = (a0_ref, a1_ref)

        def expert_scores(ii):
            a_refs[ii % 2][...] = lax.dot_general(
                u_ref[ii * n_keys:(ii + 1) * n_keys, :], x, (((1,), (1,)), ((), ())),
                preferred_element_type=F32)

        expert_scores(0)
        for ii in range(rows_per_step):
            if ii + 1 < rows_per_step:
                expert_scores(ii + 1)
            cols = slice(ii * d_cols, (ii + 1) * d_cols)
            o_ref[:, cols] += jnp.dot(ga_r[...], v_ref[:, cols], preferred_element_type=F32)
            gate = jnp.zeros(a0_ref.shape, F32)
            for h in range(n_heads):
                thr = thr_ref[h, ii:ii + 1, :]
                w = e1_ref[h, ii:ii + 1, :]
                gate = gate + jnp.where(s2_ref[h] >= thr, e2_ref[h], 0.0) * w
            g = gate * _gelu_exact(a_refs[ii % 2][...])
            ga_w[:, ii * n_keys:(ii + 1) * n_keys] = g.T.astype(BF16)

    @pl.when(s % 2 == 0)
    def _():
        body(ga0_ref, ga1_ref)

    @pl.when(s % 2 == 1)
    def _():
        body(ga1_ref, ga0_ref)


def _peer_ffn_dense(hm, u, v, s2, e2, thr, e1, n_heads, n_keys, tm):
    t, d = hm.shape
    n_exp = u.shape[0]
    rows_per_step = SUBLANES
    ec = rows_per_step * n_keys
    n_chunks = n_exp // ec
    n_steps = (t // tm) * n_chunks

    def cur(s):
        sc = jnp.minimum(s, n_steps - 1)
        return sc // n_chunks, sc % n_chunks

    def prev(s):
        sp = jnp.maximum(s - 1, 0)
        return sp // n_chunks, sp % n_chunks

    tok_spec = pl.BlockSpec((n_heads, n_keys, tm), lambda s: (0, 0, cur(s)[0]))
    row_spec = pl.BlockSpec((n_heads, rows_per_step, tm), lambda s: (0, cur(s)[1], cur(s)[0]))
    return pl.pallas_call(
        functools.partial(_peer_ffn_kernel, n_heads=n_heads, n_keys=n_keys, rows_per_step=rows_per_step,
                          n_chunks=n_chunks),
        grid=(n_steps + 1,),
        in_specs=[pl.BlockSpec((tm, d), lambda s: (cur(s)[0], 0)),
                  pl.BlockSpec((ec, d), lambda s: (cur(s)[1], 0)),
                  pl.BlockSpec((ec, d), lambda s: (prev(s)[1], 0)),
                  tok_spec, tok_spec, row_spec, row_spec],
        out_specs=pl.BlockSpec((tm, d), lambda s: (prev(s)[0], 0)),
        out_shape=jax.ShapeDtypeStruct((t, d), F32),
        scratch_shapes=[pltpu.VMEM((tm, ec), BF16), pltpu.VMEM((tm, ec), BF16),
                        pltpu.VMEM((n_keys, tm), F32), pltpu.VMEM((n_keys, tm), F32)],
        compiler_params=pltpu.CompilerParams(
            dimension_semantics=("arbitrary",), vmem_limit_bytes=VMEM_LIMIT_BYTES),
        name="peer_ffn",
    )(hm, u, v, s2, e2, thr, e1)


def _peer(h, w_q, keys, u, v):
    b, l, d = h.shape
    n_heads, _, n_keys, half = keys.shape
    hm = h.reshape(b * l, d).astype(BF16)
    eye = jnp.eye(2 * n_heads, dtype=F32)
    kflat = keys.reshape(2 * n_heads, n_keys, half)
    kbd = (eye[:, None, :, None] * kflat[:, :, None, :]).reshape(2 * n_heads * n_keys, 2 * n_heads * half)
    tm_route = _pick_tile(b * l, 256)
    s2, e2, thr, e1 = _peer_route(hm, w_q.astype(BF16), kbd.astype(BF16), n_heads, n_keys, tm_route)
    tm = _pick_tile(b * l, 512)
    out = _peer_ffn_dense(hm, u, v, s2, e2, thr, e1, n_heads, n_keys, tm)
    return out.reshape(b, l, d)


def _layer_norm(x, g, b):
    mu = jnp.mean(x, -1, keepdims=True)
    var = jnp.mean(jnp.square(x - mu), -1, keepdims=True)
    return (x - mu) * lax.rsqrt(var + LN_EPS) * g + b


def _rms_norm(x, eps=1e-6):
    return x * lax.rsqrt(jnp.mean(jnp.square(x), -1, keepdims=True) + eps)


def _dwconv1d(u, w):
    k = w.shape[0]
    return lax.conv_general_dilated(u, w[:, None, :], (1,), [(k // 2, k // 2)],
                                    dimension_numbers=('NWC', 'WIO', 'NWC'),
                                    feature_group_count=u.shape[-1])


def _axial_dwconv(u, w):
    b, n, ch = u.shape
    split = ch // 2
    rows = n // GRID_W
    uh = u[..., :split].reshape(b * rows, GRID_W, split)
    yh = _dwconv1d(uh, w[:, :split]).reshape(b, n, split)
    cv = ch - split
    uv = u[..., split:].reshape(b, rows, GRID_W, cv).transpose(0, 2, 1, 3).reshape(b * GRID_W, rows, cv)
    yv = _dwconv1d(uv, w[:, split:]).reshape(b, GRID_W, rows, cv).transpose(0, 2, 1, 3).reshape(b, n, cv)
    return jnp.concatenate([yh, yv], axis=-1)


def _conv_mixer(h, w_in, w_dw, ln_g, ln_b, w_out, on_grid):
    d = h.shape[-1]
    a = _proj(h, w_in)
    u = a[..., :d] * jax.nn.sigmoid(a[..., d:])
    u = _axial_dwconv(u, w_dw) if on_grid else _dwconv1d(u, w_dw)
    u = jax.nn.silu(_layer_norm(u, ln_g, ln_b))
    return _proj(u, w_out)


def _hgrn_consts(chunk):
    n = chunk
    levels = int(math.log2(n))
    assert 1 << levels == n
    idx = np.arange(n)
    mats = np.zeros((2, n * (1 + 2 * levels) + SUBLANES, n), np.float32)
    pair = np.zeros((2, n * (levels + 1), n), np.float32)
    jj = idx[None, :]
    tt = idx[:, None]
    for d in range(2):
        mats[d, :n] = (jj <= tt) if d == 0 else (jj >= tt)
        for l in range(levels):
            m = 1 << l
            blk = idx // (2 * m)
            far = (idx // m) % 2 == 1
            mid = (blk * 2 * m + m)[:, None]
            same = blk[:, None] == blk[None, :]
            if d == 0:
                dq = far[:, None] & (jj >= mid) & (jj <= tt)
                dk = (~far)[:, None] & (jj > tt) & (jj <= mid - 1)
                pm = far[:, None] & (~far)[None, :] & same
            else:
                dq = (~far)[:, None] & (jj >= tt) & (jj <= mid - 1)
                dk = far[:, None] & (jj >= mid) & (jj < tt)
                pm = (~far)[:, None] & far[None, :] & same
            mats[d, n * (1 + l):n * (2 + l)] = dq
            mats[d, n * (1 + levels + l):n * (2 + levels + l)] = dk
            pair[d, n * l:n * (l + 1)] = pm
        mats[d, n * (1 + 2 * levels):] = 1.0
        pair[d, n * levels:] = np.eye(n)
    return mats, pair, levels


def _hgrn_scan_kernel(q_ref, v_ref, z_ref, lb_ref, m_ref, pm_ref, s0_ref, o_ref, sfin_ref, st_ref, *,
                      chunk, levels, n_chunks):
    d = pl.program_id(0)

    @pl.when(pl.program_id(3) == 0)
    def _():
        st_ref[...] = s0_ref[...]

    n = chunk
    hps = st_ref.shape[0]
    lbs = [lb_ref[:, hh * HG_DK:(hh + 1) * HG_DK] for hh in range(hps)]
    log_lbs = [jnp.log(lb) for lb in lbs]
    log_1mlbs = [jnp.log1p(-lb) for lb in lbs]
    nt = (((1,), (1,)), ((), ()))

    def body(jj, carry):
        j = jj + d * (n_chunks - 1 - 2 * jj)
        rows = pl.ds(pl.multiple_of(j * n, n), n)
        hs = range(hps)
        lanes = [slice(hh * HG_DK, (hh + 1) * HG_DK) for hh in hs]
        qc = [q_ref[rows, lanes[hh]] for hh in hs]
        qc = [q * jax.nn.sigmoid(q) for q in qc]
        vc = [v_ref[rows, lanes[hh]].astype(BF16) for hh in hs]
        zc = [z_ref[rows, lanes[hh]] for hh in hs]
        b2 = [log_1mlbs[hh] + jnp.minimum(zc[hh], 0.0) - jnp.log1p(jnp.exp(-jnp.abs(zc[hh]))) for hh in hs]
        g = [jnp.maximum(log_lbs[hh], b2[hh]) + jnp.log1p(jnp.exp(-jnp.abs(log_lbs[hh] - b2[hh]))) for hh in hs]
        kc = [(1.0 - lbs[hh]) * jax.nn.sigmoid(-zc[hh]) for hh in hs]
        parts = []
        for hh in hs:
            parts.extend(_split_bf16(g[hh]))
        e2 = jnp.dot(m_ref[...], jnp.concatenate(parts, axis=1), preferred_element_type=F32)
        e = [e2[:, 2 * hh * HG_DK:(2 * hh + 1) * HG_DK] + e2[:, (2 * hh + 1) * HG_DK:(2 * hh + 2) * HG_DK]
             for hh in hs]
        bc = [e[hh][:n] for hh in hs]
        tot = [e[hh][n * (1 + 2 * levels):n * (1 + 2 * levels) + 1] for hh in hs]
        att = [lax.dot_general(qc[hh].astype(BF16), kc[hh].astype(BF16), nt, preferred_element_type=F32)
               * pm_ref[n * levels:n * (levels + 1), :] for hh in hs]
        for l in range(levels):
            ql = [(qc[hh] * jnp.exp(e[hh][n * (1 + l):n * (2 + l)])).astype(BF16) for hh in hs]
            kl = [(kc[hh] * jnp.exp(e[hh][n * (1 + levels + l):n * (2 + levels + l)])).astype(BF16) for hh in hs]
            att = [att[hh] + lax.dot_general(ql[hh], kl[hh], nt, preferred_element_type=F32)
                   * pm_ref[n * l:n * (l + 1), :] for hh in hs]
        st = [st_ref[hh] for hh in hs]
        qbar = [(qc[hh] * jnp.exp(bc[hh])).astype(BF16) for hh in hs]
        o = [jnp.dot(att[hh].astype(BF16), vc[hh], preferred_element_type=F32) for hh in hs]
        o = [o[hh] + lax.dot_general(qbar[hh], st[hh].astype(BF16), nt, preferred_element_type=F32) for hh in hs]
        khat = [(kc[hh] * jnp.exp(tot[hh] - bc[hh])).astype(BF16) for hh in hs]
        upd = [lax.dot_general(vc[hh], khat[hh], (((0,), (0,)), ((), ())), preferred_element_type=F32)
               for hh in hs]
        for hh in hs:
            o_ref[rows, lanes[hh]] = o[hh]
            st_ref[hh] = st[hh] * jnp.exp(tot[hh]) + upd[hh]
        return carry

    lax.fori_loop(0, n_chunks, body, 0)
    sfin_ref[...] = st_ref[...]


def _hgrn_scan(p, lb, s0, n_heads, rows_cap=512):
    b, l, _ = p.shape
    d_model = n_heads * HG_DK
    mats, pair, levels = _hgrn_consts(HG_CHUNK)
    rows = min(l, rows_cap)
    assert l % rows == 0 and rows % HG_CHUNK == 0
    nblk = l // rows

    def blk(dd, c):
        return c + dd * (nblk - 1 - 2 * c)

    hps = HG_HEADS_PER_STEP
    assert n_heads % hps == 0
    hgroups = n_heads // hps
    wide = hps * HG_DK

    def col(seg):
        return pl.BlockSpec((None, rows, wide), lambda dd, bb, hh, c: (bb, blk(dd, c), seg * hgroups + hh))

    zspec = pl.BlockSpec((None, rows, wide), lambda dd, bb, hh, c: (bb, blk(dd, c), (2 + dd) * hgroups + hh))
    st_spec = pl.BlockSpec((None, None, hps, HG_DK, HG_DK), lambda dd, bb, hh, c: (dd, bb, hh, 0, 0))
    return pl.pallas_call(
        functools.partial(_hgrn_scan_kernel, chunk=HG_CHUNK, levels=levels, n_chunks=rows // HG_CHUNK),
        grid=(2, b, hgroups, nblk),
        in_specs=[col(0), col(1), zspec,
                  pl.BlockSpec((None, 1, wide), lambda dd, bb, hh, c: (dd, 0, hh)),
                  pl.BlockSpec((None,) + mats.shape[1:], lambda dd, bb, hh, c: (dd, 0, 0)),
                  pl.BlockSpec((None,) + pair.shape[1:], lambda dd, bb, hh, c: (dd, 0, 0)),
                  st_spec],
        out_specs=[pl.BlockSpec((None, None, rows, wide), lambda dd, bb, hh, c: (dd, bb, blk(dd, c), hh)),
                   st_spec],
        out_shape=[jax.ShapeDtypeStruct((2, b, l, d_model), F32),
                   jax.ShapeDtypeStruct(s0.shape, F32)],
        scratch_shapes=[pltpu.VMEM((hps, HG_DK, HG_DK), F32)],
        compiler_params=pltpu.CompilerParams(
            dimension_semantics=("parallel", "parallel", "parallel", "arbitrary"),
            vmem_limit_bytes=VMEM_LIMIT_BYTES),
        name="hgrn_scan",
    )(p, p, p, lb.reshape(2, 1, d_model), jnp.asarray(mats, BF16), jnp.asarray(pair, F32), s0)


def _hgrn_mixer(h_ctx, h_lat, w_q, w_f, w_i, w_g, norm_g, w_o, lb, ctx_out):
    d = h_lat.shape[-1]
    n_heads = d // HG_DK
    w_cat = jnp.concatenate([w_q, w_i, w_f[0], w_f[1], w_g], axis=1)

    def readout(o2, p):
        b, l, _ = p.shape
        o = (o2[0] + o2[1]).reshape(b, l, n_heads, HG_DK)
        o = _rms_norm(o).reshape(b, l, d) * norm_g
        return _proj(o * jax.nn.silu(p[..., 4 * d:]), w_o)

    p_ctx = _proj(h_ctx, w_cat)
    p_lat = _proj(h_lat, w_cat)
    s0 = jnp.zeros((2, h_lat.shape[0], n_heads, HG_DK, HG_DK), F32)
    o_ctx, s_ctx = _hgrn_scan(p_ctx, lb, s0, n_heads)
    o_lat, _ = _hgrn_scan(p_lat, lb, s_ctx, n_heads)
    y_lat = readout(o_lat, p_lat)
    y_ctx = readout(o_ctx, p_ctx) if ctx_out else None
    return y_ctx, y_lat


def _split_bf16(x):
    hi = x.astype(BF16)
    return hi, (x - hi.astype(F32)).astype(BF16)


def _dwconv_silu_kernel(x_ref, w_ref, b_ref, o_ref, pad_ref, *, taps, halo):
    l = x_ref.shape[0]
    zeros = jnp.zeros((halo, pad_ref.shape[1]), F32)
    pad_ref[0:halo, :] = zeros
    pad_ref[halo + l:2 * halo + l, :] = zeros
    pad_ref[halo:halo + l, :] = x_ref[...]
    acc = jnp.zeros(o_ref.shape, F32) + b_ref[...]
    for k in range(taps):
        start = halo + k - taps // 2
        acc = acc + pad_ref[start:start + l, :] * w_ref[k:k + 1, :]
    o_ref[...] = acc * jax.nn.sigmoid(acc)


def _dwconv_silu(p, col0, n_cols, w, bias):
    b, l, _ = p.shape
    taps = w.shape[0]
    halo = SUBLANES
    assert taps // 2 <= halo and col0 % LANES == 0
    ct = _pick_tile(n_cols, 512)
    assert col0 % ct == 0
    return pl.pallas_call(
        functools.partial(_dwconv_silu_kernel, taps=taps, halo=halo),
        grid=(b, n_cols // ct),
        in_specs=[pl.BlockSpec((None, l, ct), lambda bb, j: (bb, 0, col0 // ct + j)),
                  pl.BlockSpec((taps, ct), lambda bb, j: (0, j)),
                  pl.BlockSpec((1, ct), lambda bb, j: (0, j))],
        out_specs=pl.BlockSpec((None, l, ct), lambda bb, j: (bb, 0, j)),
        out_shape=jax.ShapeDtypeStruct((b, l, n_cols), F32),
        scratch_shapes=[pltpu.VMEM((l + 2 * halo, ct), F32)],
        compiler_params=pltpu.CompilerParams(
            dimension_semantics=("parallel", "parallel"), vmem_limit_bytes=VMEM_LIMIT_BYTES),
        name="dwconv_silu",
    )(p, w, bias.reshape(1, n_cols))


def _ssd_consts(chunk, n_heads, hpg, headdim):
    n = chunk
    width = hpg * headdim
    idx = np.arange(n)
    tri = np.stack([idx[None, :] <= idx[:, None], idx[None, :] >= idx[:, None]]).astype(np.float32)
    pos = np.arange(width) % headdim
    head = np.arange(width) // headdim
    assert headdim == n
    eye_t = (idx[:, None] == pos[None, :]).astype(np.float32)
    keep = np.stack([idx[:, None] >= pos[None, :], idx[:, None] <= pos[None, :]]).astype(np.float32)
    blockmask = (head[:, None] == head[None, :]).astype(np.float32)
    n_groups = n_heads // hpg
    expand = np.zeros((2, n_groups, 2 * n_heads, width), np.float32)
    for d in range(2):
        for g in range(n_groups):
            expand[d, g, d * n_heads + g * hpg + head, np.arange(width)] = 1.0
    return tri, eye_t, keep, blockmask, expand


def _ssd_scan_kernel(x_ref, b_ref, c_ref, dt_ref, dtb_ref, a_ref, ex_ref, tri_ref, eye_ref, keep_ref,
                     bmask_ref, s0_ref, y_ref, sfin_ref, st_ref, *, chunk, n_chunks, hpg):
    d = pl.program_id(0)

    @pl.when(pl.program_id(3) == 0)
    def _():
        st_ref[...] = s0_ref[...]

    n = chunk
    nt = (((1,), (1,)), ((), ()))
    tn = (((0,), (0,)), ((), ()))
    ones_rows = jnp.ones((SUBLANES, n), BF16)

    gps = st_ref.shape[0]
    width = x_ref.shape[1] // gps
    n_state = b_ref.shape[1] // gps

    def body(jj, carry):
        j = jj + d * (n_chunks - 1 - 2 * jj)
        rows = pl.ds(pl.multiple_of(j * n, n), n)
        gs = range(gps)
        raw = dt_ref[rows, :] + dtb_ref[...]
        dt = jnp.maximum(raw, 0.0) + jnp.log1p(jnp.exp(-jnp.abs(raw)))
        da = dt * a_ref[...]
        da_hi, da_lo = _split_bf16(da)
        acum = jnp.dot(tri_ref[...], jnp.concatenate([da_hi, da_lo], axis=1), preferred_element_type=F32)
        acum = acum[:, :da.shape[1]] + acum[:, da.shape[1]:]
        a_tot = jnp.dot(jnp.concatenate([ones_rows, ones_rows], axis=1),
                        jnp.concatenate([da_hi, da_lo], axis=0), preferred_element_type=F32)
        dt_hi, dt_lo = _split_bf16(dt)
        ac_hi, ac_lo = _split_bf16(acum)
        tw_hi, tw_lo = _split_bf16(a_tot)
        spread = jnp.concatenate([dt_hi, dt_lo, ac_hi, ac_lo, tw_hi, tw_lo], axis=0)
        x = [x_ref[rows, g * width:(g + 1) * width] for g in gs]
        bm = [b_ref[rows, g * n_state:(g + 1) * n_state].astype(BF16) for g in gs]
        cm = [c_ref[rows, g * n_state:(g + 1) * n_state].astype(BF16) for g in gs]
        ex = [jnp.dot(spread, ex_ref[g], preferred_element_type=F32) for g in gs]
        dt_w = [ex[g][0:n] + ex[g][n:2 * n] for g in gs]
        a_col = [ex[g][2 * n:3 * n] + ex[g][3 * n:4 * n] for g in gs]
        a_end = [ex[g][4 * n:4 * n + 1] + ex[g][4 * n + SUBLANES:4 * n + SUBLANES + 1] for g in gs]
        diag = [_split_bf16(a_col[g] * eye_ref[...]) for g in gs]
        a_row = [(jnp.dot(ones_rows, diag[g][0], preferred_element_type=F32)
                  + jnp.dot(ones_rows, diag[g][1], preferred_element_type=F32))[0:1] for g in gs]
        keep = keep_ref[...] > 0.5
        seg = [jnp.exp(jnp.where(keep, a_col[g] - a_row[g], -jnp.inf)) for g in gs]
        cb = [lax.dot_general(cm[g], jnp.concatenate([bm[g]] * hpg, axis=0), nt, preferred_element_type=F32)
              for g in gs]
        xdt = [x[g] * dt_w[g] for g in gs]
        xbd = [(jnp.concatenate([xdt[g]] * hpg, axis=0) * bmask_ref[...]).astype(BF16) for g in gs]
        y = [jnp.dot((cb[g] * seg[g]).astype(BF16), xbd[g], preferred_element_type=F32) for g in gs]
        st = [st_ref[g] for g in gs]
        y = [y[g] + jnp.dot(cm[g], st[g].astype(BF16), preferred_element_type=F32) * jnp.exp(a_col[g])
             for g in gs]
        xdec = [(xdt[g] * jnp.exp(a_end[g] - a_col[g])).astype(BF16) for g in gs]
        upd = [lax.dot_general(bm[g], xdec[g], tn, preferred_element_type=F32) for g in gs]
        for g in gs:
            y_ref[rows, g * width:(g + 1) * width] = y[g]
            st_ref[g] = st[g] * jnp.exp(a_end[g]) + upd[g]
        return carry

    lax.fori_loop(0, n_chunks, body, 0)
    sfin_ref[...] = st_ref[...]


def _ssd_scan(p, xbc, dt_bias, a_neg, s0, d_inner, rows_cap=512):
    b, l, _ = p.shape
    n_heads = d_inner // M_HEADDIM
    hpg = n_heads // M_GROUPS
    width = hpg * M_HEADDIM
    assert 2 * n_heads == LANES and width % LANES == 0
    tri, eye_t, keep, blockmask, expand = _ssd_consts(M_CHUNK, n_heads, hpg, M_HEADDIM)
    rows = min(l, rows_cap)
    assert l % rows == 0 and rows % M_CHUNK == 0
    nblk = l // rows
    dt_block = (p.shape[2] - 2 * n_heads) // LANES
    b_block0 = d_inner // M_DSTATE
    c_block0 = b_block0 + M_GROUPS

    def blk(dd, c):
        return c + dd * (nblk - 1 - 2 * c)

    const2 = lambda shape: pl.BlockSpec(shape, lambda dd, bb, gg, c: (0, 0))
    per_dir = lambda shape: pl.BlockSpec((None,) + shape, lambda dd, bb, gg, c: (dd, 0, 0))
    gps = M_GROUPS_PER_STEP
    assert M_GROUPS % gps == 0 and b_block0 % gps == 0 and c_block0 % gps == 0
    st_spec = pl.BlockSpec((None, None, gps, M_DSTATE, width), lambda dd, bb, gg, c: (dd, bb, gg, 0, 0))
    return pl.pallas_call(
        functools.partial(_ssd_scan_kernel, chunk=M_CHUNK, n_chunks=rows // M_CHUNK, hpg=hpg),
        grid=(2, b, M_GROUPS // gps, nblk),
        in_specs=[pl.BlockSpec((None, rows, gps * width), lambda dd, bb, gg, c: (bb, blk(dd, c), gg)),
                  pl.BlockSpec((None, rows, gps * M_DSTATE),
                               lambda dd, bb, gg, c: (bb, blk(dd, c), b_block0 // gps + gg)),
                  pl.BlockSpec((None, rows, gps * M_DSTATE),
                               lambda dd, bb, gg, c: (bb, blk(dd, c), c_block0 // gps + gg)),
                  pl.BlockSpec((None, rows, LANES), lambda dd, bb, gg, c: (bb, blk(dd, c), dt_block)),
                  const2((1, LANES)), const2((1, LANES)),
                  pl.BlockSpec((None, gps, LANES, width), lambda dd, bb, gg, c: (dd, gg, 0, 0)),
                  per_dir((M_CHUNK, M_CHUNK)), const2((M_CHUNK, width)), per_dir((M_CHUNK, width)),
                  const2((width, width)), st_spec],
        out_specs=[pl.BlockSpec((None, None, rows, gps * width), lambda dd, bb, gg, c: (dd, bb, blk(dd, c), gg)),
                   st_spec],
        out_shape=[jax.ShapeDtypeStruct((2, b, l, d_inner), F32),
                   jax.ShapeDtypeStruct(s0.shape, F32)],
        scratch_shapes=[pltpu.VMEM((gps, M_DSTATE, width), F32)],
        compiler_params=pltpu.CompilerParams(
            dimension_semantics=("parallel", "parallel", "parallel", "arbitrary"),
            vmem_limit_bytes=VMEM_LIMIT_BYTES),
        name="ssd_scan",
    )(xbc, xbc, xbc, p, dt_bias.reshape(1, LANES), a_neg.reshape(1, LANES),
      jnp.asarray(expand, BF16), jnp.asarray(tri, BF16), jnp.asarray(eye_t, F32), jnp.asarray(keep, F32),
      jnp.asarray(blockmask, F32), s0)


def _ssd_mixer(h_ctx, h_lat, w_in, conv_w, conv_b, dt_bias, a_log, d_skip, norm_g, w_out, ctx_out):
    d_inner = w_out.shape[0]
    n_heads = d_inner // M_HEADDIM
    hpg = n_heads // M_GROUPS
    conv_dim = d_inner + 2 * M_GROUPS * M_DSTATE
    a_neg = -jnp.exp(a_log)

    def project(h):
        p = _proj(h, w_in)
        return p, _dwconv_silu(p, d_inner, conv_dim, conv_w, conv_b)

    def readout(y2, p, xbc):
        b, l, _ = p.shape
        xs = xbc[..., :d_inner]
        skip = jnp.repeat(d_skip, M_HEADDIM)
        y = (y2[0] + y2[1] + skip * xs) * jax.nn.silu(p[..., :d_inner])
        y = _rms_norm(y.reshape(b, l, M_GROUPS, -1)).reshape(b, l, d_inner) * norm_g
        return _proj(y, w_out)

    p_ctx, xbc_ctx = project(h_ctx)
    p_lat, xbc_lat = project(h_lat)
    s0 = jnp.zeros((2, h_lat.shape[0], M_GROUPS, M_DSTATE, hpg * M_HEADDIM), F32)
    y_ctx2, s_ctx = _ssd_scan(p_ctx, xbc_ctx, dt_bias, a_neg, s0, d_inner)
    y_lat2, _ = _ssd_scan(p_lat, xbc_lat, dt_bias, a_neg, s_ctx, d_inner)
    y_lat = readout(y_lat2, p_lat, xbc_lat)
    y_ctx = readout(y_ctx2, p_ctx, xbc_ctx) if ctx_out else None
    return y_ctx, y_lat


def kernel(x, c, ctx, c_ctx, mod_w, mod_b, ln_g, ln_b, conv_w_in, conv_w_dw, conv_ln_g, conv_ln_b,
           conv_w_out, hg_w_q, hg_w_f, hg_w_i, hg_w_g, hg_norm_g, hg_w_o, hg_lb_logits, m_w_in, m_conv_w,
           m_conv_b, m_dt_bias, m_A_log, m_D, m_norm_g, m_w_out, peer_w_q, peer_keys, peer_u, peer_v):
    depth = mod_w.shape[0]
    batch, _, d = x.shape
    dn_alpha = (2 * depth) ** 0.25
    reads_ctx = (False, True, True)
    last_ctx = max([i for i in range(depth) if reads_ctx[i % N_MIXERS]], default=-1)
    lb_all = jnp.cumsum(jax.nn.softmax(hg_lb_logits, axis=1), axis=1)
    lb_all = lb_all - lb_all[:, :1]

    n_mod = batch + 1
    n_mod_pad = -(-n_mod // SUBLANES) * SUBLANES
    cond = jnp.concatenate([jax.nn.silu(c), jax.nn.silu(c_ctx)[None, :],
                            jnp.zeros((n_mod_pad - n_mod, d), F32)], axis=0)

    for i in range(depth):
        kind = i % N_MIXERS
        j = i // N_MIXERS
        run_ctx = i <= last_ctx
        upd_ctx = i < last_ctx
        mod = _mm(cond.astype(BF16), mod_w[i], tn_cap=1024)[:n_mod] + mod_b[i]
        m_lat = jnp.split(mod[:batch, None, :], 6, axis=-1)
        m_ctx = jnp.split(mod[batch], 6, axis=-1)
        h_lat = x * (1.0 + m_lat[1]) + m_lat[0]
        h_ctx = ctx * (1.0 + m_ctx[1]) + m_ctx[0] if run_ctx else None
        if kind == 0:
            cw = (conv_w_in[j], conv_w_dw[j], conv_ln_g[j], conv_ln_b[j], conv_w_out[j])
            y_lat = _conv_mixer(h_lat, *cw, on_grid=True)
            y_ctx = _conv_mixer(h_ctx, *cw, on_grid=False) if upd_ctx else None
        elif kind == 1:
            y_ctx, y_lat = _hgrn_mixer(h_ctx, h_lat, hg_w_q[j], hg_w_f[j], hg_w_i[j], hg_w_g[j], hg_norm_g[j],
                                       hg_w_o[j], lb_all[:, i], upd_ctx)
        else:
            y_ctx, y_lat = _ssd_mixer(h_ctx, h_lat, m_w_in[j], m_conv_w[j], m_conv_b[j], m_dt_bias[j],
                                      m_A_log[j], m_D[j], m_norm_g[j], m_w_out[j], upd_ctx)
        pu = peer_u[i].astype(BF16)
        pv = peer_v[i].astype(BF16)
        pw = (peer_w_q[i], peer_keys[i], pu, pv)
        x = _layer_norm(dn_alpha * x + m_lat[2] * y_lat, ln_g[i, 0], ln_b[i, 0])
        x = _layer_norm(dn_alpha * x + m_lat[5] * _peer(x * (1.0 + m_lat[4]) + m_lat[3], *pw),
                        ln_g[i, 1], ln_b[i, 1])
        if upd_ctx:
            ctx = _layer_norm(dn_alpha * ctx + m_ctx[2] * y_ctx, ln_g[i, 0], ln_b[i, 0])
            ctx = _layer_norm(dn_alpha * ctx + m_ctx[5] * _peer(ctx * (1.0 + m_ctx[4]) + m_ctx[3], *pw),
                              ln_g[i, 1], ln_b[i, 1])
    return x
```

```python
import functools
import math

import jax
import jax.numpy as jnp
import numpy as np
from jax import lax
from jax.experimental import pallas as pl
from jax.experimental.pallas import tpu as pltpu

F32 = jnp.float32
BF16 = jnp.bfloat16

GRID_W = 64
N_MIXERS = 3
LN_EPS = 1e-5
HG_DK = 128
HG_CHUNK = 64
HG_HEADS_PER_STEP = 4
M_HEADDIM = 64
M_GROUPS = 8
M_GROUPS_PER_STEP = 2
M_DSTATE = 128
M_CHUNK = 64
PEER_TOPK = 16

LANES = 128
SUBLANES = 8
MXU_WIDTH = 256
VMEM_LIMIT_BYTES = 56 * 1024 * 1024


def _pick_tile(n, cap):
    if n <= cap:
        return n
    best = None
    for t in range(LANES, cap + 1, LANES):
        if n % t == 0:
            best = t
    assert best is not None, (n, cap)
    return best


def _mm_kernel(a_ref, w_ref, o_ref):
    a = a_ref[...].astype(BF16)
    w = w_ref[...].astype(BF16)
    o_ref[...] = jnp.dot(a, w, preferred_element_type=F32).astype(o_ref.dtype)


def _mm(a, w, out_dtype=F32, tm_cap=1024, tn_cap=1152):
    m, k = a.shape
    k2, n = w.shape
    assert k == k2
    tm = _pick_tile(m, tm_cap) if m % SUBLANES == 0 and m > tm_cap else m
    tn = _pick_tile(n, tn_cap)
    return pl.pallas_call(
        _mm_kernel,
        grid=(m // tm, n // tn),
        in_specs=[pl.BlockSpec((tm, k), lambda i, j: (i, 0)),
                  pl.BlockSpec((k, tn), lambda i, j: (0, j))],
        out_specs=pl.BlockSpec((tm, tn), lambda i, j: (i, j)),
        out_shape=jax.ShapeDtypeStruct((m, n), out_dtype),
        compiler_params=pltpu.CompilerParams(
            dimension_semantics=("parallel", "arbitrary"), vmem_limit_bytes=VMEM_LIMIT_BYTES),
        name="proj_mm",
    )(a, w)


def _proj(h, w):
    b, l, k = h.shape
    return _mm(h.reshape(b * l, k).astype(BF16), w.astype(BF16)).reshape(b, l, w.shape[1])


def _topk_desc(x, k):
    vals = []
    for a in range(k):
        m = jnp.max(x, axis=0, keepdims=True)
        vals.append(m)
        if a + 1 < k:
            x = jnp.where(x == m, -jnp.inf, x)
    return vals


def _peer_route_kernel(h_ref, wq_ref, kbd_ref, s2_ref, e2_ref, thr_ref, e1_ref, *, n_heads, n_keys):
    q = jnp.dot(h_ref[...], wq_ref[...], preferred_element_type=F32)
    st = lax.dot_general(kbd_ref[...], q.astype(BF16), (((1,), (1,)), ((), ())),
                         preferred_element_type=F32)
    for h in range(n_heads):
        s1 = st[(2 * h) * n_keys:(2 * h + 1) * n_keys, :]
        s2 = st[(2 * h + 1) * n_keys:(2 * h + 2) * n_keys, :]
        sv1 = _topk_desc(s1, PEER_TOPK)
        sv2 = _topk_desc(s2, PEER_TOPK)
        sv2_all = jnp.concatenate(sv2, axis=0)
        cand = jnp.concatenate([sv1[a] + sv2_all for a in range(PEER_TOPK)], axis=0)
        th = _topk_desc(cand, PEER_TOPK)[-1]
        top = sv1[0] + sv2[0]
        z = jnp.sum(jnp.where(cand >= th, jnp.exp(cand - top), 0.0), axis=0, keepdims=True)
        thr = jnp.full(s1.shape, jnp.inf, F32)
        for b in range(PEER_TOPK):
            thr = jnp.where(s1 + sv2[b] >= th, sv2[b], thr)
        s2_ref[h] = s2
        e2_ref[h] = jnp.exp(s2 - sv2[0])
        thr_ref[h] = thr
        e1_ref[h] = jnp.exp(s1 - sv1[0]) / z


def _peer_route(hm, wq, kbd, n_heads, n_keys, tm):
    t, d = hm.shape
    out_sd = jax.ShapeDtypeStruct((n_heads, n_keys, t), F32)
    out_spec = pl.BlockSpec((n_heads, n_keys, tm), lambda i: (0, 0, i))
    return pl.pallas_call(
        functools.partial(_peer_route_kernel, n_heads=n_heads, n_keys=n_keys),
        grid=(t // tm,),
        in_specs=[pl.BlockSpec((tm, d), lambda i: (i, 0)),
                  pl.BlockSpec(wq.shape, lambda i: (0, 0)),
                  pl.BlockSpec(kbd.shape, lambda i: (0, 0))],
        out_specs=[out_spec] * 4,
        out_shape=[out_sd] * 4,
        compiler_params=pltpu.CompilerParams(
            dimension_semantics=("parallel",), vmem_limit_bytes=VMEM_LIMIT_BYTES),
        name="peer_route",
    )(hm, wq, kbd)


def _gelu_exact(x):
    return 0.5 * x * (1.0 + lax.erf(x * (1.0 / math.sqrt(2.0))))


def _peer_ffn_kernel(x_ref, u_ref, v_ref, s2_ref, e2_ref, thr_ref, e1_ref, o_ref, ga0_ref, ga1_ref, a0_ref, a1_ref, *,
                     n_heads, n_keys, rows_per_step, n_chunks):
    s = pl.program_id(0)

    @pl.when(s == 0)
    def _():
        for ref in (ga0_ref, ga1_ref, a0_ref, a1_ref):
            ref[...] = jnp.zeros_like(ref)

    @pl.when(jnp.maximum(s - 2, 0) % n_chunks == 0)
    def _():
        o_ref[...] = jnp.zeros_like(o_ref)

    tm = x_ref.shape[0]

    def body(a_w, a_r, ga_w, ga_r):
        a_w[...] = lax.dot_general(u_ref[...], x_ref[...], (((1,), (1,)), ((), ())),
                                   preferred_element_type=F32)
        o_ref[...] += jnp.dot(ga_r[...], v_ref[...], preferred_element_type=F32)
        for ii in range(rows_per_step):
            lo, hi = ii * n_keys, (ii + 1) * n_keys
            gate = jnp.zeros((n_keys, tm), F32)
            for h in range(n_heads):
                thr = thr_ref[h, ii:ii + 1, :]
                w = e1_ref[h, ii:ii + 1, :]
                gate = gate + jnp.where(s2_ref[h] >= thr, e2_ref[h], 0.0) * w
            g = gate * _gelu_exact(a_r[lo:hi, :])
            ga_w[:, lo:hi] = g.T.astype(BF16)

    @pl.when(s % 2 == 0)
    def _():
        body(a0_ref, a1_ref, ga1_ref, ga0_ref)

    @pl.when(s % 2 == 1)
    def _():
        body(a1_ref, a0_ref, ga0_ref, ga1_ref)


def _peer_ffn_dense(hm, u, v, s2, e2, thr, e1, n_heads, n_keys, tm):
    t, d = hm.shape
    n_exp = u.shape[0]
    rows_per_step = SUBLANES
    ec = rows_per_step * n_keys
    n_chunks = n_exp // ec
    n_steps = (t // tm) * n_chunks

    def pair(s, lag):
        sc = jnp.clip(s - lag, 0, n_steps - 1)
        return sc // n_chunks, sc % n_chunks

    tok_spec = pl.BlockSpec((n_heads, n_keys, tm), lambda s: (0, 0, pair(s, 1)[0]))
    row_spec = pl.BlockSpec((n_heads, rows_per_step, tm), lambda s: (0, pair(s, 1)[1], pair(s, 1)[0]))
    return pl.pallas_call(
        functools.partial(_peer_ffn_kernel, n_heads=n_heads, n_keys=n_keys, rows_per_step=rows_per_step,
                          n_chunks=n_chunks),
        grid=(n_steps + 2,),
        in_specs=[pl.BlockSpec((tm, d), lambda s: (pair(s, 0)[0], 0)),
                  pl.BlockSpec((ec, d), lambda s: (pair(s, 0)[1], 0)),
                  pl.BlockSpec((ec, d), lambda s: (pair(s, 2)[1], 0)),
                  tok_spec, tok_spec, row_spec, row_spec],
        out_specs=pl.BlockSpec((tm, d), lambda s: (pair(s, 2)[0], 0)),
        out_shape=jax.ShapeDtypeStruct((t, d), F32),
        scratch_shapes=[pltpu.VMEM((tm, ec), BF16), pltpu.VMEM((tm, ec), BF16),
                        pltpu.VMEM((ec, tm), F32), pltpu.VMEM((ec, tm), F32)],
        compiler_params=pltpu.CompilerParams(
            dimension_semantics=("arbitrary",), vmem_limit_bytes=VMEM_LIMIT_BYTES),
        name="peer_ffn",
    )(hm, u, v, s2, e2, thr, e1)


def _peer(h, w_q, keys, u, v):
    b, l, d = h.shape
    n_heads, _, n_keys, half = keys.shape
    hm = h.reshape(b * l, d).astype(BF16)
    eye = jnp.eye(2 * n_heads, dtype=F32)
    kflat = keys.reshape(2 * n_heads, n_keys, half)
    kbd = (eye[:, None, :, None] * kflat[:, :, None, :]).reshape(2 * n_heads * n_keys, 2 * n_heads * half)
    tm_route = _pick_tile(b * l, 256)
    s2, e2, thr, e1 = _peer_route(hm, w_q.astype(BF16), kbd.astype(BF16), n_heads, n_keys, tm_route)
    tm = _pick_tile(b * l, 512)
    out = _peer_ffn_dense(hm, u, v, s2, e2, thr, e1, n_heads, n_keys, tm)
    return out.reshape(b, l, d)


def _layer_norm(x, g, b):
    mu = jnp.mean(x, -1, keepdims=True)
    var = jnp.mean(jnp.square(x - mu), -1, keepdims=True)
    return (x - mu) * lax.rsqrt(var + LN_EPS) * g + b


def _rms_norm(x, eps=1e-6):
    return x * lax.rsqrt(jnp.mean(jnp.square(x), -1, keepdims=True) + eps)


def _dwconv1d(u, w):
    k = w.shape[0]
    return lax.conv_general_dilated(u, w[:, None, :], (1,), [(k // 2, k // 2)],
                                    dimension_numbers=('NWC', 'WIO', 'NWC'),
                                    feature_group_count=u.shape[-1])


def _axial_dwconv(u, w):
    b, n, ch = u.shape
    split = ch // 2
    rows = n // GRID_W
    uh = u[..., :split].reshape(b * rows, GRID_W, split)
    yh = _dwconv1d(uh, w[:, :split]).reshape(b, n, split)
    cv = ch - split
    uv = u[..., split:].reshape(b, rows, GRID_W, cv).transpose(0, 2, 1, 3).reshape(b * GRID_W, rows, cv)
    yv = _dwconv1d(uv, w[:, split:]).reshape(b, GRID_W, rows, cv).transpose(0, 2, 1, 3).reshape(b, n, cv)
    return jnp.concatenate([yh, yv], axis=-1)


def _conv_mixer(h, w_in, w_dw, ln_g, ln_b, w_out, on_grid):
    d = h.shape[-1]
    a = _proj(h, w_in)
    u = a[..., :d] * jax.nn.sigmoid(a[..., d:])
    u = _axial_dwconv(u, w_dw) if on_grid else _dwconv1d(u, w_dw)
    u = jax.nn.silu(_layer_norm(u, ln_g, ln_b))
    return _proj(u, w_out)


def _hgrn_consts(chunk):
    n = chunk
    levels = int(math.log2(n))
    assert 1 << levels == n
    idx = np.arange(n)
    mats = np.zeros((2, n * (1 + 2 * levels) + SUBLANES, n), np.float32)
    pair = np.zeros((2, n * (levels + 1), n), np.float32)
    jj = idx[None, :]
    tt = idx[:, None]
    for d in range(2):
        mats[d, :n] = (jj <= tt) if d == 0 else (jj >= tt)
        for l in range(levels):
            m = 1 << l
            blk = idx // (2 * m)
            far = (idx // m) % 2 == 1
            mid = (blk * 2 * m + m)[:, None]
            same = blk[:, None] == blk[None, :]
            if d == 0:
                dq = far[:, None] & (jj >= mid) & (jj <= tt)
                dk = (~far)[:, None] & (jj > tt) & (jj <= mid - 1)
                pm = far[:, None] & (~far)[None, :] & same
            else:
                dq = (~far)[:, None] & (jj >= tt) & (jj <= mid - 1)
                dk = far[:, None] & (jj >= mid) & (jj < tt)
                pm = (~far)[:, None] & far[None, :] & same
            mats[d, n * (1 + l):n * (2 + l)] = dq
            mats[d, n * (1 + levels + l):n * (2 + levels + l)] = dk
            pair[d, n * l:n * (l + 1)] = pm
        mats[d, n * (1 + 2 * levels):] = 1.0
        pair[d, n * levels:] = np.eye(n)
    return mats, pair, levels


def _hgrn_scan_kernel(q_ref, v_ref, z_ref, lb_ref, m_ref, pm_ref, s0_ref, o_ref, sfin_ref, st_ref, *,
                      chunk, levels, n_chunks):
    d = pl.program_id(0)

    @pl.when(pl.program_id(3) == 0)
    def _():
        st_ref[...] = s0_ref[...]

    n = chunk
    hps = st_ref.shape[0]
    lbs = [lb_ref[:, hh * HG_DK:(hh + 1) * HG_DK] for hh in range(hps)]
    log_lbs = [jnp.log(lb) for lb in lbs]
    log_1mlbs = [jnp.log1p(-lb) for lb in lbs]
    nt = (((1,), (1,)), ((), ()))

    def body(jj, carry):
        j = jj + d * (n_chunks - 1 - 2 * jj)
        rows = pl.ds(pl.multiple_of(j * n, n), n)
        hs = range(hps)
        lanes = [slice(hh * HG_DK, (hh + 1) * HG_DK) for hh in hs]
        qc = [q_ref[rows, lanes[hh]] for hh in hs]
        qc = [q * jax.nn.sigmoid(q) for q in qc]
        vc = [v_ref[rows, lanes[hh]].astype(BF16) for hh in hs]
        zc = [z_ref[rows, lanes[hh]] for hh in hs]
        b2 = [log_1mlbs[hh] + jnp.minimum(zc[hh], 0.0) - jnp.log1p(jnp.exp(-jnp.abs(zc[hh]))) for hh in hs]
        g = [jnp.maximum(log_lbs[hh], b2[hh]) + jnp.log1p(jnp.exp(-jnp.abs(log_lbs[hh] - b2[hh]))) for hh in hs]
        kc = [(1.0 - lbs[hh]) * jax.nn.sigmoid(-zc[hh]) for hh in hs]
        parts = []
        for hh in hs:
            parts.extend(_split_bf16(g[hh]))
        e2 = jnp.dot(m_ref[...], jnp.concatenate(parts, axis=1), preferred_element_type=F32)
        e = [e2[:, 2 * hh * HG_DK:(2 * hh + 1) * HG_DK] + e2[:, (2 * hh + 1) * HG_DK:(2 * hh + 2) * HG_DK]
             for hh in hs]
        bc = [e[hh][:n] for hh in hs]
        tot = [e[hh][n * (1 + 2 * levels):n * (1 + 2 * levels) + 1] for hh in hs]
        att = [lax.dot_general(qc[hh].astype(BF16), kc[hh].astype(BF16), nt, preferred_element_type=F32)
               * pm_ref[n * levels:n * (levels + 1), :] for hh in hs]
        for l in range(levels):
            ql = [(qc[hh] * jnp.exp(e[hh][n * (1 + l):n * (2 + l)])).astype(BF16) for hh in hs]
            kl = [(kc[hh] * jnp.exp(e[hh][n * (1 + levels + l):n * (2 + levels + l)])).astype(BF16) for hh in hs]
            att = [att[hh] + lax.dot_general(ql[hh], kl[hh], nt, preferred_element_type=F32)
                   * pm_ref[n * l:n * (l + 1), :] for hh in hs]
        st = [st_ref[hh] for hh in hs]
        qbar = [(qc[hh] * jnp.exp(bc[hh])).astype(BF16) for hh in hs]
        o = [jnp.dot(att[hh].astype(BF16), vc[hh], preferred_element_type=F32) for hh in hs]
        o = [o[hh] + lax.dot_general(qbar[hh], st[hh].astype(BF16), nt, preferred_element_type=F32) for hh in hs]
        khat = [(kc[hh] * jnp.exp(tot[hh] - bc[hh])).astype(BF16) for hh in hs]
        upd = [lax.dot_general(vc[hh], khat[hh], (((0,), (0,)), ((), ())), preferred_element_type=F32)
               for hh in hs]
        for hh in hs:
            o_ref[rows, lanes[hh]] = o[hh]
            st_ref[hh] = st[hh] * jnp.exp(tot[hh]) + upd[hh]
        return carry

    lax.fori_loop(0, n_chunks, body, 0)
    sfin_ref[...] = st_ref[...]


def _hgrn_scan(p, lb, s0, n_heads, rows_cap=512):
    b, l, _ = p.shape
    d_model = n_heads * HG_DK
    mats, pair, levels = _hgrn_consts(HG_CHUNK)
    rows = min(l, rows_cap)
    assert l % rows == 0 and rows % HG_CHUNK == 0
    nblk = l // rows

    def blk(dd, c):
        return c + dd * (nblk - 1 - 2 * c)

    hps = HG_HEADS_PER_STEP
    assert n_heads % hps == 0
    hgroups = n_heads // hps
    wide = hps * HG_DK

    def col(seg):
        return pl.BlockSpec((None, rows, wide), lambda dd, bb, hh, c: (bb, blk(dd, c), seg * hgroups + hh))

    zspec = pl.BlockSpec((None, rows, wide), lambda dd, bb, hh, c: (bb, blk(dd, c), (2 + dd) * hgroups + hh))
    st_spec = pl.BlockSpec((None, None, hps, HG_DK, HG_DK), lambda dd, bb, hh, c: (dd, bb, hh, 0, 0))
    return pl.pallas_call(
        functools.partial(_hgrn_scan_kernel, chunk=HG_CHUNK, levels=levels, n_chunks=rows // HG_CHUNK),
        grid=(2, b, hgroups, nblk),
        in_specs=[col(0), col(1), zspec,
                  pl.BlockSpec((None, 1, wide), lambda dd, bb, hh, c: (dd, 0, hh)),
                  pl.BlockSpec((None,) + mats.shape[1:], lambda dd, bb, hh, c: (dd, 0, 0)),
                  pl.BlockSpec((None,) + pair.shape[1:], lambda dd, bb, hh, c: (dd, 0, 0)),
                  st_spec],
        out_specs=[pl.BlockSpec((None, None, rows, wide), lambda dd, bb, hh, c: (dd, bb, blk(dd, c), hh)),
                   st_spec],
        out_shape=[jax.ShapeDtypeStruct((2, b, l, d_model), F32),
                   jax.ShapeDtypeStruct(s0.shape, F32)],
        scratch_shapes=[pltpu.VMEM((hps, HG_DK, HG_DK), F32)],
        compiler_params=pltpu.CompilerParams(
            dimension_semantics=("parallel", "parallel", "parallel", "arbitrary"),
            vmem_limit_bytes=VMEM_LIMIT_BYTES),
        name="hgrn_scan",
    )(p, p, p, lb.reshape(2, 1, d_model), jnp.asarray(mats, BF16), jnp.asarray(pair, F32), s0)


def _hgrn_mixer(h_ctx, h_lat, w_q, w_f, w_i, w_g, norm_g, w_o, lb, ctx_out):
    d = h_lat.shape[-1]
    n_heads = d // HG_DK
    w_cat = jnp.concatenate([w_q, w_i, w_f[0], w_f[1], w_g], axis=1)

    def readout(o2, p):
        b, l, _ = p.shape
        o = (o2[0] + o2[1]).reshape(b, l, n_heads, HG_DK)
        o = _rms_norm(o).reshape(b, l, d) * norm_g
        return _proj(o * jax.nn.silu(p[..., 4 * d:]), w_o)

    p_ctx = _proj(h_ctx, w_cat)
    p_lat = _proj(h_lat, w_cat)
    s0 = jnp.zeros((2, h_lat.shape[0], n_heads, HG_DK, HG_DK), F32)
    o_ctx, s_ctx = _hgrn_scan(p_ctx, lb, s0, n_heads)
    o_lat, _ = _hgrn_scan(p_lat, lb, s_ctx, n_heads)
    y_lat = readout(o_lat, p_lat)
    y_ctx = readout(o_ctx, p_ctx) if ctx_out else None
    return y_ctx, y_lat


def _split_bf16(x):
    hi = x.astype(BF16)
    return hi, (x - hi.astype(F32)).astype(BF16)


def _dwconv_silu_kernel(x_ref, w_ref, b_ref, o_ref, pad_ref, *, taps, halo):
    l = x_ref.shape[0]
    zeros = jnp.zeros((halo, pad_ref.shape[1]), F32)
    pad_ref[0:halo, :] = zeros
    pad_ref[halo + l:2 * halo + l, :] = zeros
    pad_ref[halo:halo + l, :] = x_ref[...]
    acc = jnp.zeros(o_ref.shape, F32) + b_ref[...]
    for k in range(taps):
        start = halo + k - taps // 2
        acc = acc + pad_ref[start:start + l, :] * w_ref[k:k + 1, :]
    o_ref[...] = acc * jax.nn.sigmoid(acc)


def _dwconv_silu(p, col0, n_cols, w, bias):
    b, l, _ = p.shape
    taps = w.shape[0]
    halo = SUBLANES
    assert taps // 2 <= halo and col0 % LANES == 0
    ct = _pick_tile(n_cols, 512)
    assert col0 % ct == 0
    return pl.pallas_call(
        functools.partial(_dwconv_silu_kernel, taps=taps, halo=halo),
        grid=(b, n_cols // ct),
        in_specs=[pl.BlockSpec((None, l, ct), lambda bb, j: (bb, 0, col0 // ct + j)),
                  pl.BlockSpec((taps, ct), lambda bb, j: (0, j)),
                  pl.BlockSpec((1, ct), lambda bb, j: (0, j))],
        out_specs=pl.BlockSpec((None, l, ct), lambda bb, j: (bb, 0, j)),
        out_shape=jax.ShapeDtypeStruct((b, l, n_cols), F32),
        scratch_shapes=[pltpu.VMEM((l + 2 * halo, ct), F32)],
        compiler_params=pltpu.CompilerParams(
            dimension_semantics=("parallel", "parallel"), vmem_limit_bytes=VMEM_LIMIT_BYTES),
        name="dwconv_silu",
    )(p, w, bias.reshape(1, n_cols))


def _ssd_consts(chunk, n_heads, hpg, headdim):
    n = chunk
    width = hpg * headdim
    idx = np.arange(n)
    tri = np.stack([idx[None, :] <= idx[:, None], idx[None, :] >= idx[:, None]]).astype(np.float32)
    pos = np.arange(width) % headdim
    head = np.arange(width) // headdim
    assert headdim == n
    eye_t = (idx[:, None] == pos[None, :]).astype(np.float32)
    keep = np.stack([idx[:, None] >= pos[None, :], idx[:, None] <= pos[None, :]]).astype(np.float32)
    blockmask = (head[:, None] == head[None, :]).astype(np.float32)
    n_groups = n_heads // hpg
    expand = np.zeros((2, n_groups, 2 * n_heads, width), np.float32)
    for d in range(2):
        for g in range(n_groups):
            expand[d, g, d * n_heads + g * hpg + head, np.arange(width)] = 1.0
    return tri, eye_t, keep, blockmask, expand


def _ssd_scan_kernel(x_ref, b_ref, c_ref, dt_ref, dtb_ref, a_ref, ex_ref, tri_ref, eye_ref, keep_ref,
                     bmask_ref, s0_ref, y_ref, sfin_ref, st_ref, *, chunk, n_chunks, hpg):
    d = pl.program_id(0)

    @pl.when(pl.program_id(3) == 0)
    def _():
        st_ref[...] = s0_ref[...]

    n = chunk
    nt = (((1,), (1,)), ((), ()))
    tn = (((0,), (0,)), ((), ()))
    ones_rows = jnp.ones((SUBLANES, n), BF16)

    gps = st_ref.shape[0]
    width = x_ref.shape[1] // gps
    n_state = b_ref.shape[1] // gps

    def body(jj, carry):
        j = jj + d * (n_chunks - 1 - 2 * jj)
        rows = pl.ds(pl.multiple_of(j * n, n), n)
        gs = range(gps)
        raw = dt_ref[rows, :] + dtb_ref[...]
        dt = jnp.maximum(raw, 0.0) + jnp.log1p(jnp.exp(-jnp.abs(raw)))
        da = dt * a_ref[...]
        da_hi, da_lo = _split_bf16(da)
        acum = jnp.dot(tri_ref[...], jnp.concatenate([da_hi, da_lo], axis=1), preferred_element_type=F32)
        acum = acum[:, :da.shape[1]] + acum[:, da.shape[1]:]
        a_tot = jnp.dot(jnp.concatenate([ones_rows, ones_rows], axis=1),
                        jnp.concatenate([da_hi, da_lo], axis=0), preferred_element_type=F32)
        dt_hi, dt_lo = _split_bf16(dt)
        ac_hi, ac_lo = _split_bf16(acum)
        tw_hi, tw_lo = _split_bf16(a_tot)
        spread = jnp.concatenate([dt_hi, dt_lo, ac_hi, ac_lo, tw_hi, tw_lo], axis=0)
        x = [x_ref[rows, g * width:(g + 1) * width] for g in gs]
        bm = [b_ref[rows, g * n_state:(g + 1) * n_state].astype(BF16) for g in gs]
        cm = [c_ref[rows, g * n_state:(g + 1) * n_state].astype(BF16) for g in gs]
        ex = [jnp.dot(spread, ex_ref[g], preferred_element_type=F32) for g in gs]
        dt_w = [ex[g][0:n] + ex[g][n:2 * n] for g in gs]
        a_col = [ex[g][2 * n:3 * n] + ex[g][3 * n:4 * n] for g in gs]
        a_end = [ex[g][4 * n:4 * n + 1] + ex[g][4 * n + SUBLANES:4 * n + SUBLANES + 1] for g in gs]
        diag = [_split_bf16(a_col[g] * eye_ref[...]) for g in gs]
        a_row = [(jnp.dot(ones_rows, diag[g][0], preferred_element_type=F32)
                  + jnp.dot(ones_rows, diag[g][1], preferred_element_type=F32))[0:1] for g in gs]
        keep = keep_ref[...] > 0.5
        seg = [jnp.exp(jnp.where(keep, a_col[g] - a_row[g], -jnp.inf)) for g in gs]
        cb = [lax.dot_general(cm[g], jnp.concatenate([bm[g]] * hpg, axis=0), nt, preferred_element_type=F32)
              for g in gs]
        xdt = [x[g] * dt_w[g] for g in gs]
        xbd = [(jnp.concatenate([xdt[g]] * hpg, axis=0) * bmask_ref[...]).astype(BF16) for g in gs]
        y = [jnp.dot((cb[g] * seg[g]).astype(BF16), xbd[g], preferred_element_type=F32) for g in gs]
        st = [st_ref[g] for g in gs]
        y = [y[g] + jnp.dot(cm[g], st[g].astype(BF16), preferred_element_type=F32) * jnp.exp(a_col[g])
             for g in gs]
        xdec = [(xdt[g] * jnp.exp(a_end[g] - a_col[g])).astype(BF16) for g in gs]
        upd = [lax.dot_general(bm[g], xdec[g], tn, preferred_element_type=F32) for g in gs]
        for g in gs:
            y_ref[rows, g * width:(g + 1) * width] = y[g]
            st_ref[g] = st[g] * jnp.exp(a_end[g]) + upd[g]
        return carry

    lax.fori_loop(0, n_chunks, body, 0)
    sfin_ref[...] = st_ref[...]


def _ssd_scan(p, xbc, dt_bias, a_neg, s0, d_inner, rows_cap=512):
    b, l, _ = p.shape
    n_heads = d_inner // M_HEADDIM
    hpg = n_heads // M_GROUPS
    width = hpg * M_HEADDIM
    assert 2 * n_heads == LANES and width % LANES == 0
    tri, eye_t, keep, blockmask, expand = _ssd_consts(M_CHUNK, n_heads, hpg, M_HEADDIM)
    rows = min(l, rows_cap)
    assert l % rows == 0 and rows % M_CHUNK == 0
    nblk = l // rows
    dt_block = (p.shape[2] - 2 * n_heads) // LANES
    b_block0 = d_inner // M_DSTATE
    c_block0 = b_block0 + M_GROUPS

    def blk(dd, c):
        return c + dd * (nblk - 1 - 2 * c)

    const2 = lambda shape: pl.BlockSpec(shape, lambda dd, bb, gg, c: (0, 0))
    per_dir = lambda shape: pl.BlockSpec((None,) + shape, lambda dd, bb, gg, c: (dd, 0, 0))
    gps = M_GROUPS_PER_STEP
    assert M_GROUPS % gps == 0 and b_block0 % gps == 0 and c_block0 % gps == 0
    st_spec = pl.BlockSpec((None, None, gps, M_DSTATE, width), lambda dd, bb, gg, c: (dd, bb, gg, 0, 0))
    return pl.pallas_call(
        functools.partial(_ssd_scan_kernel, chunk=M_CHUNK, n_chunks=rows // M_CHUNK, hpg=hpg),
        grid=(2, b, M_GROUPS // gps, nblk),
        in_specs=[pl.BlockSpec((None, rows, gps * width), lambda dd, bb, gg, c: (bb, blk(dd, c), gg)),
                  pl.BlockSpec((None, rows, gps * M_DSTATE),
                               lambda dd, bb, gg, c: (bb, blk(dd, c), b_block0 // gps + gg)),
                  pl.BlockSpec((None, rows, gps * M_DSTATE),
                               lambda dd, bb, gg, c: (bb, blk(dd, c), c_block0 // gps + gg)),
                  pl.BlockSpec((None, rows, LANES), lambda dd, bb, gg, c: (bb, blk(dd, c), dt_block)),
                  const2((1, LANES)), const2((1, LANES)),
                  pl.BlockSpec((None, gps, LANES, width), lambda dd, bb, gg, c: (dd, gg, 0, 0)),
                  per_dir((M_CHUNK, M_CHUNK)), const2((M_CHUNK, width)), per_dir((M_CHUNK, width)),
                  const2((width, width)), st_spec],
        out_specs=[pl.BlockSpec((None, None, rows, gps * width), lambda dd, bb, gg, c: (dd, bb, blk(dd, c), gg)),
                   st_spec],
        out_shape=[jax.ShapeDtypeStruct((2, b, l, d_inner), F32),
                   jax.ShapeDtypeStruct(s0.shape, F32)],
        scratch_shapes=[pltpu.VMEM((gps, M_DSTATE, width), F32)],
        compiler_params=pltpu.CompilerParams(
            dimension_semantics=("parallel", "parallel", "parallel", "arbitrary"),
            vmem_limit_bytes=VMEM_LIMIT_BYTES),
        name="ssd_scan",
    )(xbc, xbc, xbc, p, dt_bias.reshape(1, LANES), a_neg.reshape(1, LANES),
      jnp.asarray(expand, BF16), jnp.asarray(tri, BF16), jnp.asarray(eye_t, F32), jnp.asarray(keep, F32),
      jnp.asarray(blockmask, F32), s0)


def _ssd_mixer(h_ctx, h_lat, w_in, conv_w, conv_b, dt_bias, a_log, d_skip, norm_g, w_out, ctx_out):
    d_inner = w_out.shape[0]
    n_heads = d_inner // M_HEADDIM
    hpg = n_heads // M_GROUPS
    conv_dim = d_inner + 2 * M_GROUPS * M_DSTATE
    a_neg = -jnp.exp(a_log)

    def project(h):
        p = _proj(h, w_in)
        return p, _dwconv_silu(p, d_inner, conv_dim, conv_w, conv_b)

    def readout(y2, p, xbc):
        b, l, _ = p.shape
        xs = xbc[..., :d_inner]
        skip = jnp.repeat(d_skip, M_HEADDIM)
        y = (y2[0] + y2[1] + skip * xs) * jax.nn.silu(p[..., :d_inner])
        y = _rms_norm(y.reshape(b, l, M_GROUPS, -1)).reshape(b, l, d_inner) * norm_g
        return _proj(y, w_out)

    p_ctx, xbc_ctx = project(h_ctx)
    p_lat, xbc_lat = project(h_lat)
    s0 = jnp.zeros((2, h_lat.shape[0], M_GROUPS, M_DSTATE, hpg * M_HEADDIM), F32)
    y_ctx2, s_ctx = _ssd_scan(p_ctx, xbc_ctx, dt_bias, a_neg, s0, d_inner)
    y_lat2, _ = _ssd_scan(p_lat, xbc_lat, dt_bias, a_neg, s_ctx, d_inner)
    y_lat = readout(y_lat2, p_lat, xbc_lat)
    y_ctx = readout(y_ctx2, p_ctx, xbc_ctx) if ctx_out else None
    return y_ctx, y_lat


def kernel(x, c, ctx, c_ctx, mod_w, mod_b, ln_g, ln_b, conv_w_in, conv_w_dw, conv_ln_g, conv_ln_b,
           conv_w_out, hg_w_q, hg_w_f, hg_w_i, hg_w_g, hg_norm_g, hg_w_o, hg_lb_logits, m_w_in, m_conv_w,
           m_conv_b, m_dt_bias, m_A_log, m_D, m_norm_g, m_w_out, peer_w_q, peer_keys, peer_u, peer_v):
    depth = mod_w.shape[0]
    batch, _, d = x.shape
    dn_alpha = (2 * depth) ** 0.25
    reads_ctx = (False, True, True)
    last_ctx = max([i for i in range(depth) if reads_ctx[i % N_MIXERS]], default=-1)
    lb_all = jnp.cumsum(jax.nn.softmax(hg_lb_logits, axis=1), axis=1)
    lb_all = lb_all - lb_all[:, :1]

    n_mod = batch + 1
    n_mod_pad = -(-n_mod // SUBLANES) * SUBLANES
    cond = jnp.concatenate([jax.nn.silu(c), jax.nn.silu(c_ctx)[None, :],
                            jnp.zeros((n_mod_pad - n_mod, d), F32)], axis=0)

    for i in range(depth):
        kind = i % N_MIXERS
        j = i // N_MIXERS
        run_ctx = i <= last_ctx
        upd_ctx = i < last_ctx
        mod = _mm(cond.astype(BF16), mod_w[i], tn_cap=1024)[:n_mod] + mod_b[i]
        m_lat = jnp.split(mod[:batch, None, :], 6, axis=-1)
        m_ctx = jnp.split(mod[batch], 6, axis=-1)
        h_lat = x * (1.0 + m_lat[1]) + m_lat[0]
        h_ctx = ctx * (1.0 + m_ctx[1]) + m_ctx[0] if run_ctx else None
        if kind == 0:
            cw = (conv_w_in[j], conv_w_dw[j], conv_ln_g[j], conv_ln_b[j], conv_w_out[j])
            y_lat = _conv_mixer(h_lat, *cw, on_grid=True)
            y_ctx = _conv_mixer(h_ctx, *cw, on_grid=False) if upd_ctx else None
        elif kind == 1:
            y_ctx, y_lat = _hgrn_mixer(h_ctx, h_lat, hg_w_q[j], hg_w_f[j], hg_w_i[j], hg_w_g[j], hg_norm_g[j],
                                       hg_w_o[j], lb_all[:, i], upd_ctx)
        else:
            y_ctx, y_lat = _ssd_mixer(h_ctx, h_lat, m_w_in[j], m_conv_w[j], m_conv_b[j], m_dt_bias[j],
                                      m_A_log[j], m_D[j], m_norm_g[j], m_w_out[j], upd_ctx)
        pu = peer_u[i].astype(BF16)
        pv = peer_v[i].astype(BF16)
        pw = (peer_w_q[i], peer_keys[i], pu, pv)
        x = _layer_norm(dn_alpha * x + m_lat[2] * y_lat, ln_g[i, 0], ln_b[i, 0])
        x = _layer_norm(dn_alpha * x + m_lat[5] * _peer(x * (1.0 + m_lat[4]) + m_lat[3], *pw),
                        ln_g[i, 1], ln_b[i, 1])
        if upd_ctx:
            ctx = _layer_norm(dn_alpha * ctx + m_ctx[2] * y_ctx, ln_g[i, 0], ln_b[i, 0])
            ctx = _layer_norm(dn_alpha * ctx + m_ctx[5] * _peer(ctx * (1.0 + m_ctx[4]) + m_ctx[3], *pw),
                              ln_g[i, 1], ln_b[i, 1])
    return x
```

```python
import functools
import math

import jax
import jax.numpy as jnp
import numpy as np
from jax import lax
from jax.experimental import pallas as pl
from jax.experimental.pallas import tpu as pltpu

F32 = jnp.float32
BF16 = jnp.bfloat16

GRID_W = 64
N_MIXERS = 3
LN_EPS = 1e-5
HG_DK = 128
HG_CHUNK = 64
HG_HEADS_PER_STEP = 8
M_HEADDIM = 64
M_GROUPS = 8
M_GROUPS_PER_STEP = 4
M_DSTATE = 128
M_CHUNK = 64
PEER_TOPK = 16

LANES = 128
SUBLANES = 8
MXU_WIDTH = 256
VMEM_LIMIT_BYTES = 56 * 1024 * 1024


def _pick_tile(n, cap):
    if n <= cap:
        return n
    best = None
    for t in range(LANES, cap + 1, LANES):
        if n % t == 0:
            best = t
    assert best is not None, (n, cap)
    return best


def _mm_kernel(a_ref, w_ref, o_ref):
    a = a_ref[...].astype(BF16)
    w = w_ref[...].astype(BF16)
    o_ref[...] = jnp.dot(a, w, preferred_element_type=F32).astype(o_ref.dtype)


def _mm(a, w, out_dtype=F32, tm_cap=1024, tn_cap=1152):
    m, k = a.shape
    k2, n = w.shape
    assert k == k2
    tm = _pick_tile(m, tm_cap) if m % SUBLANES == 0 and m > tm_cap else m
    tn = _pick_tile(n, tn_cap)
    return pl.pallas_call(
        _mm_kernel,
        grid=(m // tm, n // tn),
        in_specs=[pl.BlockSpec((tm, k), lambda i, j: (i, 0)),
                  pl.BlockSpec((k, tn), lambda i, j: (0, j))],
        out_specs=pl.BlockSpec((tm, tn), lambda i, j: (i, j)),
        out_shape=jax.ShapeDtypeStruct((m, n), out_dtype),
        compiler_params=pltpu.CompilerParams(
            dimension_semantics=("parallel", "arbitrary"), vmem_limit_bytes=VMEM_LIMIT_BYTES),
        name="proj_mm",
    )(a, w)


def _proj(h, w):
    b, l, k = h.shape
    return _mm(h.reshape(b * l, k).astype(BF16), w.astype(BF16)).reshape(b, l, w.shape[1])


def _topk_desc(x, k):
    vals = []
    for a in range(k):
        m = jnp.max(x, axis=0, keepdims=True)
        vals.append(m)
        if a + 1 < k:
            x = jnp.where(x == m, -jnp.inf, x)
    return vals


def _peer_route_kernel(h_ref, wq_ref, kbd_ref, s2_ref, e2_ref, thr_ref, e1_ref, *, n_heads, n_keys):
    q = jnp.dot(h_ref[...], wq_ref[...], preferred_element_type=F32)
    st = lax.dot_general(kbd_ref[...], q.astype(BF16), (((1,), (1,)), ((), ())),
                         preferred_element_type=F32)
    for h in range(n_heads):
        s1 = st[(2 * h) * n_keys:(2 * h + 1) * n_keys, :]
        s2 = st[(2 * h + 1) * n_keys:(2 * h + 2) * n_keys, :]
        sv1 = _topk_desc(s1, PEER_TOPK)
        sv2 = _topk_desc(s2, PEER_TOPK)
        sv2_all = jnp.concatenate(sv2, axis=0)
        cand = jnp.concatenate(
            [sv1[a] + sv2_all[:min(PEER_TOPK, -(-(PEER_TOPK // (a + 1)) // SUBLANES) * SUBLANES)]
             for a in range(PEER_TOPK)], axis=0)
        th = _topk_desc(cand, PEER_TOPK)[-1]
        top = sv1[0] + sv2[0]
        z = jnp.sum(jnp.where(cand >= th, jnp.exp(cand - top), 0.0), axis=0, keepdims=True)
        thr = jnp.full(s1.shape, jnp.inf, F32)
        for b in range(PEER_TOPK):
            thr = jnp.where(s1 + sv2[b] >= th, sv2[b], thr)
        s2_ref[h] = s2
        e2_ref[h] = jnp.exp(s2 - sv2[0])
        thr_ref[h] = thr
        e1_ref[h] = jnp.exp(s1 - sv1[0]) / z


def _peer_route(hm, wq, kbd, n_heads, n_keys, tm):
    t, d = hm.shape
    out_sd = jax.ShapeDtypeStruct((n_heads, n_keys, t), F32)
    out_spec = pl.BlockSpec((n_heads, n_keys, tm), lambda i: (0, 0, i))
    return pl.pallas_call(
        functools.partial(_peer_route_kernel, n_heads=n_heads, n_keys=n_keys),
        grid=(t // tm,),
        in_specs=[pl.BlockSpec((tm, d), lambda i: (i, 0)),
                  pl.BlockSpec(wq.shape, lambda i: (0, 0)),
                  pl.BlockSpec(kbd.shape, lambda i: (0, 0))],
        out_specs=[out_spec] * 4,
        out_shape=[out_sd] * 4,
        compiler_params=pltpu.CompilerParams(
            dimension_semantics=("parallel",), vmem_limit_bytes=VMEM_LIMIT_BYTES),
        name="peer_route",
    )(hm, wq, kbd)


def _gelu_exact(x):
    return 0.5 * x * (1.0 + lax.erf(x * (1.0 / math.sqrt(2.0))))


def _peer_ffn_kernel(x_ref, u_ref, v_ref, s2_ref, e2_ref, thr_ref, e1_ref, o_ref, ga0_ref, ga1_ref, a0_ref, a1_ref, *,
                     n_heads, n_keys, rows_per_step, n_chunks):
    s = pl.program_id(0)

    @pl.when(s == 0)
    def _():
        for ref in (ga0_ref, ga1_ref, a0_ref, a1_ref):
            ref[...] = jnp.zeros_like(ref)

    @pl.when(jnp.maximum(s - 2, 0) % n_chunks == 0)
    def _():
        o_ref[...] = jnp.zeros_like(o_ref)

    tm = x_ref.shape[0]

    def body(a_w, a_r, ga_w, ga_r):
        a_w[...] = lax.dot_general(u_ref[...], x_ref[...], (((1,), (1,)), ((), ())),
                                   preferred_element_type=F32)
        o_ref[...] += jnp.dot(ga_r[...], v_ref[...], preferred_element_type=F32)
        for ii in range(rows_per_step):
            lo, hi = ii * n_keys, (ii + 1) * n_keys
            gate = jnp.zeros((n_keys, tm), F32)
            for h in range(n_heads):
                thr = thr_ref[h, ii:ii + 1, :]
                w = e1_ref[h, ii:ii + 1, :]
                gate = gate + jnp.where(s2_ref[h] >= thr, e2_ref[h], 0.0) * w
            g = gate * _gelu_exact(a_r[lo:hi, :])
            ga_w[:, lo:hi] = g.T.astype(BF16)

    @pl.when(s % 2 == 0)
    def _():
        body(a0_ref, a1_ref, ga1_ref, ga0_ref)

    @pl.when(s % 2 == 1)
    def _():
        body(a1_ref, a0_ref, ga0_ref, ga1_ref)


def _peer_ffn_dense(hm, u, v, s2, e2, thr, e1, n_heads, n_keys, tm):
    t, d = hm.shape
    n_exp = u.shape[0]
    rows_per_step = SUBLANES
    ec = rows_per_step * n_keys
    n_chunks = n_exp // ec
    n_steps = (t // tm) * n_chunks

    def pair(s, lag):
        sc = jnp.clip(s - lag, 0, n_steps - 1)
        return sc // n_chunks, sc % n_chunks

    tok_spec = pl.BlockSpec((n_heads, n_keys, tm), lambda s: (0, 0, pair(s, 1)[0]))
    row_spec = pl.BlockSpec((n_heads, rows_per_step, tm), lambda s: (0, pair(s, 1)[1], pair(s, 1)[0]))
    return pl.pallas_call(
        functools.partial(_peer_ffn_kernel, n_heads=n_heads, n_keys=n_keys, rows_per_step=rows_per_step,
                          n_chunks=n_chunks),
        grid=(n_steps + 2,),
        in_specs=[pl.BlockSpec((tm, d), lambda s: (pair(s, 0)[0], 0)),
                  pl.BlockSpec((ec, d), lambda s: (pair(s, 0)[1], 0)),
                  pl.BlockSpec((ec, d), lambda s: (pair(s, 2)[1], 0)),
                  tok_spec, tok_spec, row_spec, row_spec],
        out_specs=pl.BlockSpec((tm, d), lambda s: (pair(s, 2)[0], 0)),
        out_shape=jax.ShapeDtypeStruct((t, d), F32),
        scratch_shapes=[pltpu.VMEM((tm, ec), BF16), pltpu.VMEM((tm, ec), BF16),
                        pltpu.VMEM((ec, tm), F32), pltpu.VMEM((ec, tm), F32)],
        compiler_params=pltpu.CompilerParams(
            dimension_semantics=("arbitrary",), vmem_limit_bytes=VMEM_LIMIT_BYTES),
        name="peer_ffn",
    )(hm, u, v, s2, e2, thr, e1)


def _peer(h, w_q, keys, u, v):
    b, l, d = h.shape
    n_heads, _, n_keys, half = keys.shape
    hm = h.reshape(b * l, d).astype(BF16)
    eye = jnp.eye(2 * n_heads, dtype=F32)
    kflat = keys.reshape(2 * n_heads, n_keys, half)
    kbd = (eye[:, None, :, None] * kflat[:, :, None, :]).reshape(2 * n_heads * n_keys, 2 * n_heads * half)
    tm_route = _pick_tile(b * l, 256)
    s2, e2, thr, e1 = _peer_route(hm, w_q.astype(BF16), kbd.astype(BF16), n_heads, n_keys, tm_route)
    tm = _pick_tile(b * l, 512)
    out = _peer_ffn_dense(hm, u, v, s2, e2, thr, e1, n_heads, n_keys, tm)
    return out.reshape(b, l, d)


def _layer_norm(x, g, b):
    mu = jnp.mean(x, -1, keepdims=True)
    var = jnp.mean(jnp.square(x - mu), -1, keepdims=True)
    return (x - mu) * lax.rsqrt(var + LN_EPS) * g + b


def _rms_norm(x, eps=1e-6):
    return x * lax.rsqrt(jnp.mean(jnp.square(x), -1, keepdims=True) + eps)


def _dwconv1d(u, w):
    k = w.shape[0]
    return lax.conv_general_dilated(u, w[:, None, :], (1,), [(k // 2, k // 2)],
                                    dimension_numbers=('NWC', 'WIO', 'NWC'),
                                    feature_group_count=u.shape[-1])


def _axial_dwconv(u, w):
    b, n, ch = u.shape
    split = ch // 2
    rows = n // GRID_W
    uh = u[..., :split].reshape(b * rows, GRID_W, split)
    yh = _dwconv1d(uh, w[:, :split]).reshape(b, n, split)
    cv = ch - split
    uv = u[..., split:].reshape(b, rows, GRID_W, cv).transpose(0, 2, 1, 3).reshape(b * GRID_W, rows, cv)
    yv = _dwconv1d(uv, w[:, split:]).reshape(b, GRID_W, rows, cv).transpose(0, 2, 1, 3).reshape(b, n, cv)
    return jnp.concatenate([yh, yv], axis=-1)


def _conv_mixer(h, w_in, w_dw, ln_g, ln_b, w_out, on_grid):
    d = h.shape[-1]
    a = _proj(h, w_in)
    u = a[..., :d] * jax.nn.sigmoid(a[..., d:])
    u = _axial_dwconv(u, w_dw) if on_grid else _dwconv1d(u, w_dw)
    u = jax.nn.silu(_layer_norm(u, ln_g, ln_b))
    return _proj(u, w_out)


def _hgrn_consts(chunk):
    n = chunk
    levels = int(math.log2(n))
    assert 1 << levels == n
    idx = np.arange(n)
    mats = np.zeros((2, n * (1 + 2 * levels) + SUBLANES, n), np.float32)
    pair = np.zeros((2, n * (levels + 1), n), np.float32)
    jj = idx[None, :]
    tt = idx[:, None]
    for d in range(2):
        mats[d, :n] = (jj <= tt) if d == 0 else (jj >= tt)
        for l in range(levels):
            m = 1 << l
            blk = idx // (2 * m)
            far = (idx // m) % 2 == 1
            mid = (blk * 2 * m + m)[:, None]
            same = blk[:, None] == blk[None, :]
            if d == 0:
                dq = far[:, None] & (jj >= mid) & (jj <= tt)
                dk = (~far)[:, None] & (jj > tt) & (jj <= mid - 1)
                pm = far[:, None] & (~far)[None, :] & same
            else:
                dq = (~far)[:, None] & (jj >= tt) & (jj <= mid - 1)
                dk = far[:, None] & (jj >= mid) & (jj < tt)
                pm = (~far)[:, None] & far[None, :] & same
            mats[d, n * (1 + l):n * (2 + l)] = dq
            mats[d, n * (1 + levels + l):n * (2 + levels + l)] = dk
            pair[d, n * l:n * (l + 1)] = pm
        mats[d, n * (1 + 2 * levels):] = 1.0
        pair[d, n * levels:] = np.eye(n)
    return mats, pair, levels


def _hgrn_scan_kernel(q_ref, v_ref, z_ref, lb_ref, m_ref, pm_ref, s0_ref, o_ref, sfin_ref, st_ref, *,
                      chunk, levels, n_chunks):
    d = pl.program_id(0)

    @pl.when(pl.program_id(3) == 0)
    def _():
        st_ref[...] = s0_ref[...]

    n = chunk
    hps = st_ref.shape[0]
    lbs = [lb_ref[:, hh * HG_DK:(hh + 1) * HG_DK] for hh in range(hps)]
    log_lbs = [jnp.log(lb) for lb in lbs]
    log_1mlbs = [jnp.log1p(-lb) for lb in lbs]
    nt = (((1,), (1,)), ((), ()))

    def body(jj, carry):
        j = jj + d * (n_chunks - 1 - 2 * jj)
        rows = pl.ds(pl.multiple_of(j * n, n), n)
        hs = range(hps)
        lanes = [slice(hh * HG_DK, (hh + 1) * HG_DK) for hh in hs]
        qc = [q_ref[rows, lanes[hh]] for hh in hs]
        qc = [q * jax.nn.sigmoid(q) for q in qc]
        vc = [v_ref[rows, lanes[hh]].astype(BF16) for hh in hs]
        zc = [z_ref[rows, lanes[hh]] for hh in hs]
        b2 = [log_1mlbs[hh] + jnp.minimum(zc[hh], 0.0) - jnp.log1p(jnp.exp(-jnp.abs(zc[hh]))) for hh in hs]
        g = [jnp.maximum(log_lbs[hh], b2[hh]) + jnp.log1p(jnp.exp(-jnp.abs(log_lbs[hh] - b2[hh]))) for hh in hs]
        kc = [(1.0 - lbs[hh]) * jax.nn.sigmoid(-zc[hh]) for hh in hs]
        parts = []
        for hh in hs:
            parts.extend(_split_bf16(g[hh]))
        e2 = jnp.dot(m_ref[...], jnp.concatenate(parts, axis=1), preferred_element_type=F32)
        e = [e2[:, 2 * hh * HG_DK:(2 * hh + 1) * HG_DK] + e2[:, (2 * hh + 1) * HG_DK:(2 * hh + 2) * HG_DK]
             for hh in hs]
        bc = [e[hh][:n] for hh in hs]
        tot = [e[hh][n * (1 + 2 * levels):n * (1 + 2 * levels) + 1] for hh in hs]
        att = [lax.dot_general(qc[hh].astype(BF16), kc[hh].astype(BF16), nt, preferred_element_type=F32)
               * pm_ref[n * levels:n * (levels + 1), :] for hh in hs]
        for l in range(levels):
            ql = [(qc[hh] * jnp.exp(e[hh][n * (1 + l):n * (2 + l)])).astype(BF16) for hh in hs]
            kl = [(kc[hh] * jnp.exp(e[hh][n * (1 + levels + l):n * (2 + levels + l)])).astype(BF16) for hh in hs]
            att = [att[hh] + lax.dot_general(ql[hh], kl[hh], nt, preferred_element_type=F32)
                   * pm_ref[n * l:n * (l + 1), :] for hh in hs]
        st = [st_ref[hh] for hh in hs]
        qbar = [(qc[hh] * jnp.exp(bc[hh])).astype(BF16) for hh in hs]
        o = [jnp.dot(att[hh].astype(BF16), vc[hh], preferred_element_type=F32) for hh in hs]
        o = [o[hh] + lax.dot_general(qbar[hh], st[hh].astype(BF16), nt, preferred_element_type=F32) for hh in hs]
        khat = [(kc[hh] * jnp.exp(tot[hh] - bc[hh])).astype(BF16) for hh in hs]
        upd = [lax.dot_general(vc[hh], khat[hh], (((0,), (0,)), ((), ())), preferred_element_type=F32)
               for hh in hs]
        for hh in hs:
            o_ref[rows, lanes[hh]] = o[hh]
            st_ref[hh] = st[hh] * jnp.exp(tot[hh]) + upd[hh]
        return carry

    lax.fori_loop(0, n_chunks, body, 0)
    sfin_ref[...] = st_ref[...]


def _hgrn_scan(p, lb, s0, n_heads, rows_cap=512):
    b, l, _ = p.shape
    d_model = n_heads * HG_DK
    mats, pair, levels = _hgrn_consts(HG_CHUNK)
    rows = min(l, rows_cap)
    assert l % rows == 0 and rows % HG_CHUNK == 0
    nblk = l // rows

    def blk(dd, c):
        return c + dd * (nblk - 1 - 2 * c)

    hps = HG_HEADS_PER_STEP
    assert n_heads % hps == 0
    hgroups = n_heads // hps
    wide = hps * HG_DK

    def col(seg):
        return pl.BlockSpec((None, rows, wide), lambda dd, bb, hh, c: (bb, blk(dd, c), seg * hgroups + hh))

    zspec = pl.BlockSpec((None, rows, wide), lambda dd, bb, hh, c: (bb, blk(dd, c), (2 + dd) * hgroups + hh))
    st_spec = pl.BlockSpec((None, None, hps, HG_DK, HG_DK), lambda dd, bb, hh, c: (dd, bb, hh, 0, 0))
    return pl.pallas_call(
        functools.partial(_hgrn_scan_kernel, chunk=HG_CHUNK, levels=levels, n_chunks=rows // HG_CHUNK),
        grid=(2, b, hgroups, nblk),
        in_specs=[col(0), col(1), zspec,
                  pl.BlockSpec((None, 1, wide), lambda dd, bb, hh, c: (dd, 0, hh)),
                  pl.BlockSpec((None,) + mats.shape[1:], lambda dd, bb, hh, c: (dd, 0, 0)),
                  pl.BlockSpec((None,) + pair.shape[1:], lambda dd, bb, hh, c: (dd, 0, 0)),
                  st_spec],
        out_specs=[pl.BlockSpec((None, None, rows, wide), lambda dd, bb, hh, c: (dd, bb, blk(dd, c), hh)),
                   st_spec],
        out_shape=[jax.ShapeDtypeStruct((2, b, l, d_model), F32),
                   jax.ShapeDtypeStruct(s0.shape, F32)],
        scratch_shapes=[pltpu.VMEM((hps, HG_DK, HG_DK), F32)],
        compiler_params=pltpu.CompilerParams(
            dimension_semantics=("parallel", "parallel", "parallel", "arbitrary"),
            vmem_limit_bytes=VMEM_LIMIT_BYTES),
        name="hgrn_scan",
    )(p, p, p, lb.reshape(2, 1, d_model), jnp.asarray(mats, BF16), jnp.asarray(pair, F32), s0)


def _hgrn_mixer(h_ctx, h_lat, w_q, w_f, w_i, w_g, norm_g, w_o, lb, ctx_out):
    d = h_lat.shape[-1]
    n_heads = d // HG_DK
    w_cat = jnp.concatenate([w_q, w_i, w_f[0], w_f[1], w_g], axis=1)

    def readout(o2, p):
        b, l, _ = p.shape
        o = (o2[0] + o2[1]).reshape(b, l, n_heads, HG_DK)
        o = _rms_norm(o).reshape(b, l, d) * norm_g
        return _proj(o * jax.nn.silu(p[..., 4 * d:]), w_o)

    p_ctx = _proj(h_ctx, w_cat)
    p_lat = _proj(h_lat, w_cat)
    s0 = jnp.zeros((2, h_lat.shape[0], n_heads, HG_DK, HG_DK), F32)
    o_ctx, s_ctx = _hgrn_scan(p_ctx, lb, s0, n_heads)
    o_lat, _ = _hgrn_scan(p_lat, lb, s_ctx, n_heads)
    y_lat = readout(o_lat, p_lat)
    y_ctx = readout(o_ctx, p_ctx) if ctx_out else None
    return y_ctx, y_lat


def _split_bf16(x):
    hi = x.astype(BF16)
    return hi, (x - hi.astype(F32)).astype(BF16)


def _dwconv_silu_kernel(x_ref, w_ref, b_ref, o_ref, pad_ref, *, taps, halo):
    l = x_ref.shape[0]
    zeros = jnp.zeros((halo, pad_ref.shape[1]), F32)
    pad_ref[0:halo, :] = zeros
    pad_ref[halo + l:2 * halo + l, :] = zeros
    pad_ref[halo:halo + l, :] = x_ref[...]
    acc = jnp.zeros(o_ref.shape, F32) + b_ref[...]
    for k in range(taps):
        start = halo + k - taps // 2
        acc = acc + pad_ref[start:start + l, :] * w_ref[k:k + 1, :]
    o_ref[...] = acc * jax.nn.sigmoid(acc)


def _dwconv_silu(p, col0, n_cols, w, bias):
    b, l, _ = p.shape
    taps = w.shape[0]
    halo = SUBLANES
    assert taps // 2 <= halo and col0 % LANES == 0
    ct = _pick_tile(n_cols, 512)
    assert col0 % ct == 0
    return pl.pallas_call(
        functools.partial(_dwconv_silu_kernel, taps=taps, halo=halo),
        grid=(b, n_cols // ct),
        in_specs=[pl.BlockSpec((None, l, ct), lambda bb, j: (bb, 0, col0 // ct + j)),
                  pl.BlockSpec((taps, ct), lambda bb, j: (0, j)),
                  pl.BlockSpec((1, ct), lambda bb, j: (0, j))],
        out_specs=pl.BlockSpec((None, l, ct), lambda bb, j: (bb, 0, j)),
        out_shape=jax.ShapeDtypeStruct((b, l, n_cols), F32),
        scratch_shapes=[pltpu.VMEM((l + 2 * halo, ct), F32)],
        compiler_params=pltpu.CompilerParams(
            dimension_semantics=("parallel", "parallel"), vmem_limit_bytes=VMEM_LIMIT_BYTES),
        name="dwconv_silu",
    )(p, w, bias.reshape(1, n_cols))


def _ssd_consts(chunk, n_heads, hpg, headdim):
    n = chunk
    width = hpg * headdim
    idx = np.arange(n)
    tri = np.stack([idx[None, :] <= idx[:, None], idx[None, :] >= idx[:, None]]).astype(np.float32)
    pos = np.arange(width) % headdim
    head = np.arange(width) // headdim
    assert headdim == n
    eye_t = (idx[:, None] == pos[None, :]).astype(np.float32)
    keep = np.stack([idx[:, None] >= pos[None, :], idx[:, None] <= pos[None, :]]).astype(np.float32)
    blockmask = (head[:, None] == head[None, :]).astype(np.float32)
    n_groups = n_heads // hpg
    expand = np.zeros((2, n_groups, 2 * n_heads, width), np.float32)
    for d in range(2):
        for g in range(n_groups):
            expand[d, g, d * n_heads + g * hpg + head, np.arange(width)] = 1.0
    return tri, eye_t, keep, blockmask, expand


def _ssd_scan_kernel(x_ref, b_ref, c_ref, dt_ref, dtb_ref, a_ref, ex_ref, tri_ref, eye_ref, keep_ref,
                     bmask_ref, s0_ref, y_ref, sfin_ref, st_ref, *, chunk, n_chunks, hpg):
    d = pl.program_id(0)

    @pl.when(pl.program_id(3) == 0)
    def _():
        st_ref[...] = s0_ref[...]

    n = chunk
    nt = (((1,), (1,)), ((), ()))
    tn = (((0,), (0,)), ((), ()))
    ones_rows = jnp.ones((SUBLANES, n), BF16)

    gps = st_ref.shape[0]
    width = x_ref.shape[1] // gps
    n_state = b_ref.shape[1] // gps

    def body(jj, carry):
        j = jj + d * (n_chunks - 1 - 2 * jj)
        rows = pl.ds(pl.multiple_of(j * n, n), n)
        gs = range(gps)
        raw = dt_ref[rows, :] + dtb_ref[...]
        dt = jnp.maximum(raw, 0.0) + jnp.log1p(jnp.exp(-jnp.abs(raw)))
        da = dt * a_ref[...]
        da_hi, da_lo = _split_bf16(da)
        acum = jnp.dot(tri_ref[...], jnp.concatenate([da_hi, da_lo], axis=1), preferred_element_type=F32)
        acum = acum[:, :da.shape[1]] + acum[:, da.shape[1]:]
        a_tot = jnp.dot(jnp.concatenate([ones_rows, ones_rows], axis=1),
                        jnp.concatenate([da_hi, da_lo], axis=0), preferred_element_type=F32)
        dt_hi, dt_lo = _split_bf16(dt)
        ac_hi, ac_lo = _split_bf16(acum)
        tw_hi, tw_lo = _split_bf16(a_tot)
        spread = jnp.concatenate([dt_hi, dt_lo, ac_hi, ac_lo, tw_hi, tw_lo], axis=0)
        x = [x_ref[rows, g * width:(g + 1) * width] for g in gs]
        bm = [b_ref[rows, g * n_state:(g + 1) * n_state].astype(BF16) for g in gs]
        cm = [c_ref[rows, g * n_state:(g + 1) * n_state].astype(BF16) for g in gs]
        ex = [jnp.dot(spread, ex_ref[g], preferred_element_type=F32) for g in gs]
        dt_w = [ex[g][0:n] + ex[g][n:2 * n] for g in gs]
        a_col = [ex[g][2 * n:3 * n] + ex[g][3 * n:4 * n] for g in gs]
        a_end = [ex[g][4 * n:4 * n + 1] + ex[g][4 * n + SUBLANES:4 * n + SUBLANES + 1] for g in gs]
        diag = [_split_bf16(a_col[g] * eye_ref[...]) for g in gs]
        a_row = [(jnp.dot(ones_rows, diag[g][0], preferred_element_type=F32)
                  + jnp.dot(ones_rows, diag[g][1], preferred_element_type=F32))[0:1] for g in gs]
        keep = keep_ref[...] > 0.5
        seg = [jnp.exp(jnp.where(keep, a_col[g] - a_row[g], -jnp.inf)) for g in gs]
        cb = [lax.dot_general(cm[g], jnp.concatenate([bm[g]] * hpg, axis=0), nt, preferred_element_type=F32)
              for g in gs]
        xdt = [x[g] * dt_w[g] for g in gs]
        xbd = [(jnp.concatenate([xdt[g]] * hpg, axis=0) * bmask_ref[...]).astype(BF16) for g in gs]
        y = [jnp.dot((cb[g] * seg[g]).astype(BF16), xbd[g], preferred_element_type=F32) for g in gs]
        st = [st_ref[g] for g in gs]
        y = [y[g] + jnp.dot(cm[g], st[g].astype(BF16), preferred_element_type=F32) * jnp.exp(a_col[g])
             for g in gs]
        xdec = [(xdt[g] * jnp.exp(a_end[g] - a_col[g])).astype(BF16) for g in gs]
        upd = [lax.dot_general(bm[g], xdec[g], tn, preferred_element_type=F32) for g in gs]
        for g in gs:
            y_ref[rows, g * width:(g + 1) * width] = y[g]
            st_ref[g] = st[g] * jnp.exp(a_end[g]) + upd[g]
        return carry

    lax.fori_loop(0, n_chunks, body, 0)
    sfin_ref[...] = st_ref[...]


def _ssd_scan(p, xbc, dt_bias, a_neg, s0, d_inner, rows_cap=512):
    b, l, _ = p.shape
    n_heads = d_inner // M_HEADDIM
    hpg = n_heads // M_GROUPS
    width = hpg * M_HEADDIM
    assert 2 * n_heads == LANES and width % LANES == 0
    tri, eye_t, keep, blockmask, expand = _ssd_consts(M_CHUNK, n_heads, hpg, M_HEADDIM)
    rows = min(l, rows_cap)
    assert l % rows == 0 and rows % M_CHUNK == 0
    nblk = l // rows
    dt_block = (p.shape[2] - 2 * n_heads) // LANES
    b_block0 = d_inner // M_DSTATE
    c_block0 = b_block0 + M_GROUPS

    def blk(dd, c):
        return c + dd * (nblk - 1 - 2 * c)

    const2 = lambda shape: pl.BlockSpec(shape, lambda dd, bb, gg, c: (0, 0))
    per_dir = lambda shape: pl.BlockSpec((None,) + shape, lambda dd, bb, gg, c: (dd, 0, 0))
    gps = M_GROUPS_PER_STEP
    assert M_GROUPS % gps == 0 and b_block0 % gps == 0 and c_block0 % gps == 0
    st_spec = pl.BlockSpec((None, None, gps, M_DSTATE, width), lambda dd, bb, gg, c: (dd, bb, gg, 0, 0))
    return pl.pallas_call(
        functools.partial(_ssd_scan_kernel, chunk=M_CHUNK, n_chunks=rows // M_CHUNK, hpg=hpg),
        grid=(2, b, M_GROUPS // gps, nblk),
        in_specs=[pl.BlockSpec((None, rows, gps * width), lambda dd, bb, gg, c: (bb, blk(dd, c), gg)),
                  pl.BlockSpec((None, rows, gps * M_DSTATE),
                               lambda dd, bb, gg, c: (bb, blk(dd, c), b_block0 // gps + gg)),
                  pl.BlockSpec((None, rows, gps * M_DSTATE),
                               lambda dd, bb, gg, c: (bb, blk(dd, c), c_block0 // gps + gg)),
                  pl.BlockSpec((None, rows, LANES), lambda dd, bb, gg, c: (bb, blk(dd, c), dt_block)),
                  const2((1, LANES)), const2((1, LANES)),
                  pl.BlockSpec((None, gps, LANES, width), lambda dd, bb, gg, c: (dd, gg, 0, 0)),
                  per_dir((M_CHUNK, M_CHUNK)), const2((M_CHUNK, width)), per_dir((M_CHUNK, width)),
                  const2((width, width)), st_spec],
        out_specs=[pl.BlockSpec((None, None, rows, gps * width), lambda dd, bb, gg, c: (dd, bb, blk(dd, c), gg)),
                   st_spec],
        out_shape=[jax.ShapeDtypeStruct((2, b, l, d_inner), F32),
                   jax.ShapeDtypeStruct(s0.shape, F32)],
        scratch_shapes=[pltpu.VMEM((gps, M_DSTATE, width), F32)],
        compiler_params=pltpu.CompilerParams(
            dimension_semantics=("parallel", "parallel", "parallel", "arbitrary"),
            vmem_limit_bytes=VMEM_LIMIT_BYTES),
        name="ssd_scan",
    )(xbc, xbc, xbc, p, dt_bias.reshape(1, LANES), a_neg.reshape(1, LANES),
      jnp.asarray(expand, BF16), jnp.asarray(tri, BF16), jnp.asarray(eye_t, F32), jnp.asarray(keep, F32),
      jnp.asarray(blockmask, F32), s0)


def _ssd_mixer(h_ctx, h_lat, w_in, conv_w, conv_b, dt_bias, a_log, d_skip, norm_g, w_out, ctx_out):
    d_inner = w_out.shape[0]
    n_heads = d_inner // M_HEADDIM
    hpg = n_heads // M_GROUPS
    conv_dim = d_inner + 2 * M_GROUPS * M_DSTATE
    a_neg = -jnp.exp(a_log)

    def project(h):
        p = _proj(h, w_in)
        return p, _dwconv_silu(p, d_inner, conv_dim, conv_w, conv_b)

    def readout(y2, p, xbc):
        b, l, _ = p.shape
        xs = xbc[..., :d_inner]
        skip = jnp.repeat(d_skip, M_HEADDIM)
        y = (y2[0] + y2[1] + skip * xs) * jax.nn.silu(p[..., :d_inner])
        y = _rms_norm(y.reshape(b, l, M_GROUPS, -1)).reshape(b, l, d_inner) * norm_g
        return _proj(y, w_out)

    p_ctx, xbc_ctx = project(h_ctx)
    p_lat, xbc_lat = project(h_lat)
    s0 = jnp.zeros((2, h_lat.shape[0], M_GROUPS, M_DSTATE, hpg * M_HEADDIM), F32)
    y_ctx2, s_ctx = _ssd_scan(p_ctx, xbc_ctx, dt_bias, a_neg, s0, d_inner)
    y_lat2, _ = _ssd_scan(p_lat, xbc_lat, dt_bias, a_neg, s_ctx, d_inner)
    y_lat = readout(y_lat2, p_lat, xbc_lat)
    y_ctx = readout(y_ctx2, p_ctx, xbc_ctx) if ctx_out else None
    return y_ctx, y_lat


def kernel(x, c, ctx, c_ctx, mod_w, mod_b, ln_g, ln_b, conv_w_in, conv_w_dw, conv_ln_g, conv_ln_b,
           conv_w_out, hg_w_q, hg_w_f, hg_w_i, hg_w_g, hg_norm_g, hg_w_o, hg_lb_logits, m_w_in, m_conv_w,
           m_conv_b, m_dt_bias, m_A_log, m_D, m_norm_g, m_w_out, peer_w_q, peer_keys, peer_u, peer_v):
    depth = mod_w.shape[0]
    batch, _, d = x.shape
    dn_alpha = (2 * depth) ** 0.25
    reads_ctx = (False, True, True)
    last_ctx = max([i for i in range(depth) if reads_ctx[i % N_MIXERS]], default=-1)
    lb_all = jnp.cumsum(jax.nn.softmax(hg_lb_logits, axis=1), axis=1)
    lb_all = lb_all - lb_all[:, :1]

    n_mod = batch + 1
    n_mod_pad = -(-n_mod // SUBLANES) * SUBLANES
    cond = jnp.concatenate([jax.nn.silu(c), jax.nn.silu(c_ctx)[None, :],
                            jnp.zeros((n_mod_pad - n_mod, d), F32)], axis=0)

    for i in range(depth):
        kind = i % N_MIXERS
        j = i // N_MIXERS
        run_ctx = i <= last_ctx
        upd_ctx = i < last_ctx
        mod = _mm(cond.astype(BF16), mod_w[i], tn_cap=1024)[:n_mod] + mod_b[i]
        m_lat = jnp.split(mod[:batch, None, :], 6, axis=-1)
        m_ctx = jnp.split(mod[batch], 6, axis=-1)
        h_lat = x * (1.0 + m_lat[1]) + m_lat[0]
        h_ctx = ctx * (1.0 + m_ctx[1]) + m_ctx[0] if run_ctx else None
        if kind == 0:
            cw = (conv_w_in[j], conv_w_dw[j], conv_ln_g[j], conv_ln_b[j], conv_w_out[j])
            y_lat = _conv_mixer(h_lat, *cw, on_grid=True)
            y_ctx = _conv_mixer(h_ctx, *cw, on_grid=False) if upd_ctx else None
        elif kind == 1:
            y_ctx, y_lat = _hgrn_mixer(h_ctx, h_lat, hg_w_q[j], hg_w_f[j], hg_w_i[j], hg_w_g[j], hg_norm_g[j],
                                       hg_w_o[j], lb_all[:, i], upd_ctx)
        else:
            y_ctx, y_lat = _ssd_mixer(h_ctx, h_lat, m_w_in[j], m_conv_w[j], m_conv_b[j], m_dt_bias[j],
                                      m_A_log[j], m_D[j], m_norm_g[j], m_w_out[j], upd_ctx)
        pu = peer_u[i].astype(BF16)
        pv = peer_v[i].astype(BF16)
        pw = (peer_w_q[i], peer_keys[i], pu, pv)
        x = _layer_norm(dn_alpha * x + m_lat[2] * y_lat, ln_g[i, 0], ln_b[i, 0])
        x = _layer_norm(dn_alpha * x + m_lat[5] * _peer(x * (1.0 + m_lat[4]) + m_lat[3], *pw),
                        ln_g[i, 1], ln_b[i, 1])
        if upd_ctx:
            ctx = _layer_norm(dn_alpha * ctx + m_ctx[2] * y_ctx, ln_g[i, 0], ln_b[i, 0])
            ctx = _layer_norm(dn_alpha * ctx + m_ctx[5] * _peer(ctx * (1.0 + m_ctx[4]) + m_ctx[3], *pw),
                              ln_g[i, 1], ln_b[i, 1])
    return x
```
